```python
import math
import jax, jax.numpy as jnp
from jax import lax
import numpy as np

D_MODEL = 1024
BATCH = 4
SEQ = 4096
DEPTH = 4
DEC_BATCH = 128
DEC_SEQ = 1
PAST_LEN = 2048
PAGE_SIZE = 128

HEAD_DIM = 64
MIX_W = D_MODEL
N_ATTN_HEADS = 8
ATTN_W = N_ATTN_HEADS * HEAD_DIM
CONV_W = MIX_W - ATTN_W
CONV_K = 3
DILATIONS = ((128, 1), (512, 4), (2048, 16))
MAX_WINDOW = 2048
N_BUCKETS = 32
MAX_DISTANCE = MAX_WINDOW
BLOCK = 128
ALPHA = (2 * DEPTH) ** 0.25
BETA = (8 * DEPTH) ** -0.25
LN_EPS = 1e-5
SCALE = HEAD_DIM ** -0.5
IN_COLS = 4 * CONV_W + 4 * ATTN_W
SPLITS = [CONV_W, 2 * CONV_W, 3 * CONV_W, 4 * CONV_W,
          4 * CONV_W + ATTN_W, 4 * CONV_W + 2 * ATTN_W, 4 * CONV_W + 3 * ATTN_W]

kernel_name = "hybrid_conv_dilated_attn_deepnorm_step"


def _layer_norm(x, g, b):
    xf = x.astype(jnp.float32)
    mu = jnp.mean(xf, axis=-1, keepdims=True)
    var = jnp.mean(jnp.square(xf - mu), axis=-1, keepdims=True)
    return ((xf - mu) * lax.rsqrt(var + LN_EPS) * g + b).astype(x.dtype)


def _rel_bucket(dist):
    n_exact = N_BUCKETS // 2
    d = jnp.maximum(dist, 0)
    large = n_exact + (jnp.log(jnp.maximum(d, 1).astype(jnp.float32) / n_exact)
                       / math.log(MAX_DISTANCE / n_exact) * (N_BUCKETS - n_exact)).astype(jnp.int32)
    large = jnp.minimum(large, N_BUCKETS - 1)
    return jnp.where(d < n_exact, d, large)


def _dilated_prompt(q, k, v, rel_bias, window, dil):
    bsz, slen, nh, hd = q.shape
    n_len = slen // dil
    steps = window // dil
    nb = -(-n_len // BLOCK)
    lp = nb * BLOCK

    def to_blocks(t):
        t = t.reshape(bsz, n_len, dil, nh, hd)
        t = jnp.pad(t, ((0, 0), (0, lp - n_len), (0, 0), (0, 0), (0, 0)))
        return t.reshape(bsz, nb, BLOCK, dil, nh, hd)

    qb, kb, vb = to_blocks(q), to_blocks(k), to_blocks(v)
    pad_prev = ((0, 0), (1, 0), (0, 0), (0, 0), (0, 0), (0, 0))
    kk = jnp.concatenate([jnp.pad(kb, pad_prev)[:, :-1], kb], axis=2)
    vv = jnp.concatenate([jnp.pad(vb, pad_prev)[:, :-1], vb], axis=2)
    s = jnp.einsum('bnqrhd,bnkrhd->bnrhqk', qb, kk, preferred_element_type=jnp.float32) * SCALE
    qi = jnp.arange(BLOCK)[:, None]
    kj = jnp.arange(2 * BLOCK)[None, :]
    step = qi + BLOCK - kj
    band = (step >= 0) & (step <= steps)
    first = (jnp.arange(nb)[:, None] > 0) | (kj >= BLOCK)
    mask = band[None, :, :] & first[:, None, :]
    bias = rel_bias[_rel_bucket(jnp.clip(step, 0, steps) * dil)].astype(jnp.float32)
    s = s + jnp.transpose(bias, (2, 0, 1))
    s = jnp.where(mask[None, :, None, None], s, -jnp.inf)
    m = jnp.max(s, axis=-1)
    p = jnp.exp(s - m[..., None])
    l = jnp.sum(p, axis=-1)
    num = jnp.einsum('bnrhqk,bnkrhd->bnqrhd', p, vv.astype(jnp.float32))
    num = num.reshape(bsz, lp, dil, nh, hd)[:, :n_len].reshape(bsz, slen, nh, hd)
    unblock = lambda t: jnp.transpose(t, (0, 1, 4, 2, 3)).reshape(bsz, lp, dil, nh)[:, :n_len].reshape(bsz, slen, nh)
    return num, unblock(m), unblock(l)


def _dilated_sample(q, k_all, v_all, rel_bias, window, dil, n_past):
    ds = q.shape[1]
    steps = window // dil
    jq = jnp.arange(ds)[:, None]
    st = jnp.arange(steps + 1)[None, :]
    idx = n_past + jq - st * dil
    valid = idx >= 0
    idxc = jnp.maximum(idx, 0)
    kg = k_all[:, idxc]
    vg = v_all[:, idxc]
    s = jnp.einsum('bqhd,bqkhd->bqhk', q, kg, preferred_element_type=jnp.float32) * SCALE
    bias = rel_bias[_rel_bucket(jnp.arange(steps + 1) * dil)].astype(jnp.float32)
    s = s + bias.T
    s = jnp.where(valid[None, :, None, :], s, -jnp.inf)
    m = jnp.max(s, axis=-1)
    p = jnp.exp(s - m[..., None])
    l = jnp.sum(p, axis=-1)
    num = jnp.einsum('bqhk,bqkhd->bqhd', p, vg.astype(jnp.float32))
    return num, m, l


def _combine(branches):
    m_all = jnp.max(jnp.stack([m for _, m, _ in branches]), axis=0)
    num = 0.0
    den = 0.0
    for n, m, l in branches:
        w = jnp.exp(m - m_all)
        num = num + n * w[..., None]
        den = den + l * w
    return num / den[..., None]


def _attn_prompt(q, k, v, rel_bias):
    slen = q.shape[1]
    out = _combine([_dilated_prompt(q, k, v, rel_bias, w, d) for w, d in DILATIONS]).astype(q.dtype)
    keep = min(MAX_WINDOW, slen)
    return out, k[:, slen - keep:], v[:, slen - keep:]


def _attn_sample(q, k, v, rel_bias, cache_k_l, cache_v_l):
    n_past = cache_k_l.shape[1]
    k_all = jnp.concatenate([cache_k_l, k], axis=1)
    v_all = jnp.concatenate([cache_v_l, v], axis=1)
    out = _combine([_dilated_sample(q, k_all, v_all, rel_bias, w, d, n_past) for w, d in DILATIONS]).astype(q.dtype)
    return out, k, v


def _conv_prompt(u, w):
    slen = u.shape[1]
    up = jnp.pad(u, ((0, 0), (CONV_K - 1, 0), (0, 0)))
    y = sum(w[t] * up[:, t:t + slen] for t in range(CONV_K))
    return y, up[:, up.shape[1] - (CONV_K - 1):]


def _conv_sample(u, w, state):
    ds = u.shape[1]
    cat = jnp.concatenate([state, u], axis=1)
    y = sum(w[t] * cat[:, t:t + ds] for t in range(CONV_K))
    return y, cat[:, cat.shape[1] - (CONV_K - 1):]


def _mixer_layer(x, w_in_l, conv_w_l, w_out_l, g_l, b_l, rel_bias, conv_fn, attn_fn):
    bsz, slen = x.shape[0], x.shape[1]
    z = jnp.einsum('bsd,dc->bsc', x, w_in_l)
    cx, cb, cc, cg, q, k, v, ag = jnp.split(z, SPLITS, axis=-1)
    cy, conv_state = conv_fn(cc * cx, conv_w_l)
    conv_out = cb * cy * jax.nn.silu(cg)
    heads = lambda t: t.reshape(bsz, slen, N_ATTN_HEADS, HEAD_DIM)
    att, k_state, v_state = attn_fn(heads(q), heads(k), heads(v), rel_bias)
    attn_out = att.reshape(bsz, slen, ATTN_W) * jax.nn.silu(ag)
    mixed = jnp.concatenate([conv_out, attn_out], axis=-1)
    out = jnp.einsum('bsc,cd->bsd', mixed, w_out_l)
    y = _layer_norm(ALPHA * x + out, g_l, b_l)
    return y, k_state, v_state, conv_state


def setup_inputs(seed: int = 0) -> dict:
    key = jax.random.key(seed)
    ks = jax.random.split(key, 11)
    kv_buf = min(MAX_WINDOW, PAST_LEN)
    x_prompt = jax.random.normal(ks[0], (BATCH, SEQ, D_MODEL), jnp.float32)
    x_sample = jax.random.normal(ks[1], (DEC_BATCH, DEC_SEQ, D_MODEL), jnp.float32)
    cache_k = jax.random.normal(ks[2], (DEPTH, DEC_BATCH, kv_buf, N_ATTN_HEADS, HEAD_DIM), jnp.float32)
    cache_v = jax.random.normal(ks[3], (DEPTH, DEC_BATCH, kv_buf, N_ATTN_HEADS, HEAD_DIM), jnp.float32)
    state_conv = jax.random.normal(ks[4], (DEPTH, DEC_BATCH, CONV_K - 1, CONV_W), jnp.float32)
    w_in = jax.random.normal(ks[5], (DEPTH, D_MODEL, IN_COLS), jnp.float32) * D_MODEL ** -0.5
    conv_w = jax.random.normal(ks[6], (DEPTH, CONV_K, CONV_W), jnp.float32) * CONV_K ** -0.5
    w_out = jax.random.normal(ks[7], (DEPTH, MIX_W, D_MODEL), jnp.float32) * (MIX_W ** -0.5 * BETA)
    ln_g = 1.0 + 0.02 * jax.random.normal(ks[8], (DEPTH, D_MODEL), jnp.float32)
    ln_b = 0.02 * jax.random.normal(ks[9], (DEPTH, D_MODEL), jnp.float32)
    rel_bias = 0.5 * jax.random.normal(ks[10], (N_BUCKETS, N_ATTN_HEADS), jnp.float32)
    return {"x_prompt": x_prompt, "x_sample": x_sample, "cache_k": cache_k, "cache_v": cache_v,
            "state_conv": state_conv, "w_in": w_in, "conv_w": conv_w, "w_out": w_out,
            "ln_g": ln_g, "ln_b": ln_b, "rel_bias": rel_bias}


def reference(x_prompt, x_sample, cache_k, cache_v, state_conv, w_in, conv_w, w_out, ln_g, ln_b, rel_bias):
    hp, hs = x_prompt, x_sample
    kp_l, vp_l, cp_l, ks_l, vs_l, cs_l = [], [], [], [], [], []
    for l in range(DEPTH):
        hp, kp, vp, cp = _mixer_layer(hp, w_in[l], conv_w[l], w_out[l], ln_g[l], ln_b[l], rel_bias,
                                      _conv_prompt, _attn_prompt)
        ck, cv, sc = cache_k[l], cache_v[l], state_conv[l]
        hs, ksn, vsn, csn = _mixer_layer(
            hs, w_in[l], conv_w[l], w_out[l], ln_g[l], ln_b[l], rel_bias,
            lambda u, w, sc=sc: _conv_sample(u, w, sc),
            lambda q, k, v, rb, ck=ck, cv=cv: _attn_sample(q, k, v, rb, ck, cv))
        kp_l.append(kp); vp_l.append(vp); cp_l.append(cp)
        ks_l.append(ksn); vs_l.append(vsn); cs_l.append(csn)
    new_k_prompt = jnp.stack(kp_l)
    new_v_prompt = jnp.stack(vp_l)
    new_conv_prompt = jnp.stack(cp_l)
    new_k_sample = jnp.stack(ks_l)
    new_v_sample = jnp.stack(vs_l)
    new_conv_sample = jnp.stack(cs_l)
    return (hp, hs, new_k_prompt, new_v_prompt, new_conv_prompt, new_k_sample, new_v_sample, new_conv_sample)
```

```python
import functools
import math

import jax
import jax.numpy as jnp
from jax import lax
from jax.experimental import pallas as pl
from jax.experimental.pallas import tpu as pltpu

F32 = jnp.float32
BF16 = jnp.bfloat16

D_MODEL = 1024
HEAD_DIM = 64
N_HEADS = 8
ATTN_W = N_HEADS * HEAD_DIM
CONV_W = D_MODEL - ATTN_W
CONV_K = 3
DILATIONS = ((128, 1), (512, 4), (2048, 16))
MAX_WINDOW = 2048
N_BUCKETS = 32
MAX_DISTANCE = MAX_WINDOW
BLOCK = 128
LN_EPS = 1e-5
SCALE = HEAD_DIM ** -0.5
NEG = -1e30

LANES = 128
SEQ_TILE = 512
SAMPLE_BLOCK = 2
VMEM_LIMIT = 56 * 1024 * 1024


def _silu(x):
    return x * (1.0 / (1.0 + jnp.exp(-x)))


def _layer_norm(y, g, b):
    mu = jnp.mean(y, axis=-1, keepdims=True)
    yc = y - mu
    var = jnp.mean(yc * yc, axis=-1, keepdims=True)
    return yc * lax.rsqrt(var + LN_EPS) * g + b


def _rel_bucket(dist):
    n_exact = N_BUCKETS // 2
    d = jnp.maximum(dist, 0)
    large = n_exact + (jnp.log(jnp.maximum(d, 1).astype(F32) / n_exact)
                       / math.log(MAX_DISTANCE / n_exact) * (N_BUCKETS - n_exact)).astype(jnp.int32)
    large = jnp.minimum(large, N_BUCKETS - 1)
    return jnp.where(d < n_exact, d, large)


def _params(*sem):
    return pltpu.CompilerParams(dimension_semantics=sem, vmem_limit_bytes=VMEM_LIMIT)


def _proj_kernel(x_ref, w_ref, cw_ref, q_ref, k_ref, v_ref, g_ref, co_ref, kt_ref, vt_ref, cs_ref, ubuf,
                 *, first_kept):
    ts = x_ref.shape[0]
    s = pl.program_id(1)
    xb = x_ref[...].astype(BF16)

    def col(j):
        return jnp.dot(xb, w_ref[:, j * 512:(j + 1) * 512], preferred_element_type=F32)

    u = col(2) * col(0)

    @pl.when(s == 0)
    def _():
        ubuf[0:8, :] = jnp.zeros((8, CONV_W), F32)

    ubuf[8:8 + ts, :] = u
    cw = cw_ref[...]
    cy = cw[0:1, :] * ubuf[6:6 + ts, :] + cw[1:2, :] * ubuf[7:7 + ts, :] + cw[2:3, :] * u
    co_ref[...] = (col(1) * cy * _silu(col(3))).astype(BF16)
    cs_ref[...] = u[ts - 2:ts, :]
    ubuf[0:8, :] = u[ts - 8:ts, :]

    q_ref[...] = (col(4) * SCALE).astype(BF16)
    k = col(5)
    k_ref[...] = k.astype(BF16)
    v = col(6)
    v_ref[...] = v.astype(BF16)
    g_ref[...] = _silu(col(7)).astype(BF16)

    @pl.when(s >= first_kept)
    def _():
        kt_ref[...] = k.T
        vt_ref[...] = v.T


def _proj_call(x, w_in_l, conv_w_l, keep):
    bsz, slen, _ = x.shape
    ts = SEQ_TILE
    ns = slen // ts
    off = (slen - keep) // ts
    tile = lambda width: pl.BlockSpec((None, ts, width), lambda b, s: (b, s, 0))
    kept = pl.BlockSpec((None, ATTN_W, ts), lambda b, s: (b, 0, jnp.maximum(s - off, 0)))
    bf = jax.ShapeDtypeStruct((bsz, slen, ATTN_W), BF16)
    return pl.pallas_call(
        functools.partial(_proj_kernel, first_kept=off),
        grid=(bsz, ns),
        in_specs=[tile(D_MODEL),
                  pl.BlockSpec((D_MODEL, 4 * CONV_W + 4 * ATTN_W), lambda b, s: (0, 0)),
                  pl.BlockSpec((CONV_K, CONV_W), lambda b, s: (0, 0))],
        out_specs=[tile(ATTN_W), tile(ATTN_W), tile(ATTN_W), tile(ATTN_W), tile(CONV_W), kept, kept,
                   pl.BlockSpec((None, CONV_K - 1, CONV_W), lambda b, s: (b, 0, 0))],
        out_shape=[bf, bf, bf, bf, jax.ShapeDtypeStruct((bsz, slen, CONV_W), BF16),
                   jax.ShapeDtypeStruct((bsz, ATTN_W, keep), F32),
                   jax.ShapeDtypeStruct((bsz, ATTN_W, keep), F32),
                   jax.ShapeDtypeStruct((bsz, CONV_K - 1, CONV_W), F32)],
        scratch_shapes=[pltpu.VMEM((ts + 8, CONV_W), F32)],
        compiler_params=_params("arbitrary", "arbitrary"),
        name="prompt_proj",
    )(x, w_in_l, conv_w_l)


def _attn_kernel(q_ref, k_ref, v_ref, bias_ref, o_ref, lse_ref, vaug):
    n = q_ref.shape[0]
    nb = n // BLOCK
    n_pairs = ATTN_W // LANES

    def stage_v(rows):
        for p in range(n_pairs):
            vaug[p, rows, 0:LANES] = v_ref[rows, p * LANES:(p + 1) * LANES]
            vaug[p, rows, LANES:2 * LANES] = jnp.ones((BLOCK, LANES), BF16)

    stage_v(pl.ds(0, BLOCK))
    lane = lax.broadcasted_iota(jnp.int32, (BLOCK, LANES), 1)
    low = lane < HEAD_DIM

    def body(j, carry):
        qs = pl.multiple_of(j * BLOCK, BLOCK)
        ks = pl.multiple_of(jnp.maximum(j - 1, 0) * BLOCK, BLOCK)
        t = jnp.minimum(j, 1)
        stage_v(pl.ds(pl.multiple_of(jnp.minimum(j + 1, nb - 1) * BLOCK, BLOCK), BLOCK))
        lse_tile = jnp.zeros((BLOCK, LANES), F32)
        for p in range(n_pairs):
            cols = slice(p * LANES, (p + 1) * LANES)
            qp = q_ref[pl.ds(qs, BLOCK), cols]
            kp = k_ref[pl.ds(ks, 2 * BLOCK), cols]
            vp = vaug[p, pl.ds(ks, 2 * BLOCK), :]
            outs = []
            for hh in range(2):
                h = 2 * p + hh
                qm = jnp.where(low if hh == 0 else jnp.logical_not(low), qp, jnp.zeros_like(qp))
                sc = lax.dot_general(qm, kp, (((1,), (1,)), ((), ())), preferred_element_type=F32)
                sc = sc + bias_ref[t, h]
                m = jnp.max(sc, axis=1, keepdims=True)
                pe = jnp.exp(sc - m).astype(BF16)
                out = jnp.dot(pe, vp, preferred_element_type=F32)
                den = out[:, LANES:2 * LANES]
                outs.append(out[:, 0:LANES] / den)
                lse_tile = jnp.where(lane == h, m + jnp.log(den), lse_tile)
            o_ref[pl.ds(qs, BLOCK), cols] = jnp.where(low, outs[0], outs[1]).astype(BF16)
        lse_ref[pl.ds(qs, BLOCK), :] = lse_tile
        return carry

    lax.fori_loop(0, nb, body, 0)


def _attn_call(q, k, v, bias, dil):
    bsz, slen, _ = q.shape
    n = slen // dil
    view = lambda t: t.reshape(bsz, n, dil * ATTN_W)
    blk = pl.BlockSpec((None, n, ATTN_W), lambda b, r: (b, 0, r))
    o, lse = pl.pallas_call(
        _attn_kernel,
        grid=(bsz, dil),
        in_specs=[blk, blk, blk, pl.BlockSpec(bias.shape, lambda b, r: (0, 0, 0, 0))],
        out_specs=[blk, pl.BlockSpec((None, n, LANES), lambda b, r: (b, 0, r))],
        out_shape=[jax.ShapeDtypeStruct((bsz, n, dil * ATTN_W), BF16),
                   jax.ShapeDtypeStruct((bsz, n, dil * LANES), F32)],
        scratch_shapes=[pltpu.VMEM((ATTN_W // LANES, n, 2 * LANES), BF16)],
        compiler_params=_params("arbitrary", "arbitrary"),
        name=f"prompt_attn_d{dil}",
    )(view(q), view(k), view(v), bias)
    return o.reshape(bsz, slen, ATTN_W), lse.reshape(bsz, slen, LANES)


def _prompt_bias(rel_bias, dil):
    steps = BLOCK
    qi = jnp.arange(BLOCK)[:, None]
    kj = jnp.arange(2 * BLOCK)[None, :]
    step1 = qi + BLOCK - kj
    ok1 = (step1 >= 0) & (step1 <= steps)
    step0 = qi - kj
    ok0 = (step0 >= 0) & (kj < BLOCK)

    def table(step, ok):
        b = rel_bias[_rel_bucket(jnp.clip(step, 0, steps) * dil)].astype(F32)
        return jnp.transpose(jnp.where(ok[:, :, None], b, NEG), (2, 0, 1))

    return jnp.stack([table(step0, ok0), table(step1, ok1)])


def _out_kernel(o1_ref, o2_ref, o3_ref, l1_ref, l2_ref, l3_ref, g_ref, co_ref, x_ref, e_ref, w_ref,
                lg_ref, lb_ref, y_ref, *, alpha):
    l1, l2, l3 = l1_ref[...], l2_ref[...], l3_ref[...]
    m = jnp.maximum(jnp.maximum(l1, l2), l3)
    e1, e2, e3 = jnp.exp(l1 - m), jnp.exp(l2 - m), jnp.exp(l3 - m)
    inv = 1.0 / (e1 + e2 + e3)

    def spread(w):
        hi = w.astype(BF16)
        lo = (w - hi.astype(F32)).astype(BF16)
        return (jnp.dot(hi, e_ref[...], preferred_element_type=F32)
                + jnp.dot(lo, e_ref[...], preferred_element_type=F32))

    att = (spread(e1 * inv) * o1_ref[...].astype(F32) + spread(e2 * inv) * o2_ref[...].astype(F32)
           + spread(e3 * inv) * o3_ref[...].astype(F32))
    gated = (att * g_ref[...].astype(F32)).astype(BF16)
    out = (jnp.dot(co_ref[...], w_ref[0:CONV_W, :], preferred_element_type=F32)
           + jnp.dot(gated, w_ref[CONV_W:D_MODEL, :], preferred_element_type=F32))
    y_ref[...] = _layer_norm(alpha * x_ref[...] + out, lg_ref[...], lb_ref[...])


def _out_call(os_, lses, gate, co, x, expand, w_out_l, g_l, b_l, alpha):
    bsz, slen, _ = x.shape
    ts = SEQ_TILE
    tile = lambda width: pl.BlockSpec((None, ts, width), lambda b, s: (b, s, 0))
    full = lambda shape: pl.BlockSpec(shape, lambda b, s: (0,) * len(shape))
    return pl.pallas_call(
        functools.partial(_out_kernel, alpha=alpha),
        grid=(bsz, slen // ts),
        in_specs=[tile(ATTN_W)] * 3 + [tile(LANES)] * 3 + [tile(ATTN_W), tile(CONV_W), tile(D_MODEL),
                  full((LANES, ATTN_W)), full((D_MODEL, D_MODEL)), full((1, D_MODEL)), full((1, D_MODEL))],
        out_specs=tile(D_MODEL),
        out_shape=jax.ShapeDtypeStruct((bsz, slen, D_MODEL), F32),
        compiler_params=_params("arbitrary", "arbitrary"),
        name="prompt_out",
    )(*os_, *lses, gate, co, x, expand, w_out_l, g_l, b_l)


def _sample_proj_kernel(x_ref, w_ref, cw_ref, s0_ref, s1_ref, qt_ref, kt_ref, vt_ref, co_ref, g_ref, cs_ref):
    xb = x_ref[...].astype(BF16)

    def col(c):
        return jnp.dot(xb, w_ref[:, c * 512:(c + 1) * 512], preferred_element_type=F32)

    u = col(2) * col(0)
    cw = cw_ref[...]
    cy = cw[0:1, :] * s0_ref[...] + cw[1:2, :] * s1_ref[...] + cw[2:3, :] * u
    co_ref[...] = col(1) * cy * _silu(col(3))
    cs_ref[...] = u
    qt_ref[...] = (col(4) * SCALE).T
    kt_ref[...] = col(5).T
    vt_ref[...] = col(6).T
    g_ref[...] = _silu(col(7))


def _sample_proj_call(xs, w_in_l, conv_w_l, s0, s1):
    db = xs.shape[0]
    tr = jax.ShapeDtypeStruct((ATTN_W, db), F32)
    row = jax.ShapeDtypeStruct((db, ATTN_W), F32)
    return pl.pallas_call(
        _sample_proj_kernel,
        out_shape=[tr, tr, tr, row, row, row],
        compiler_params=pltpu.CompilerParams(vmem_limit_bytes=VMEM_LIMIT),
        name="sample_proj",
    )(xs, w_in_l, conv_w_l, s0, s1)


def _sample_attn_kernel(qt_ref, kt_ref, vt_ref, sb_ref, sb0_ref, kc_ref, vc_ref, at_ref):
    j = pl.program_id(0)
    bb = kc_ref.shape[0]
    db = qt_ref.shape[1]
    lane = lax.broadcasted_iota(jnp.int32, (HEAD_DIM, db), 1)

    @pl.when(j == 0)
    def _():
        at_ref[...] = jnp.zeros(at_ref.shape, F32)

    for bi in range(bb):
        sel = lane == j * bb + bi
        for h in range(N_HEADS):
            rows = slice(h * HEAD_DIM, (h + 1) * HEAD_DIM)
            pick = lambda ref: jnp.sum(jnp.where(sel, ref[rows, :], 0.0), axis=1, keepdims=True)
            qc, kn, vn = pick(qt_ref), pick(kt_ref), pick(vt_ref)
            sc = jnp.sum(kc_ref[bi, rows, :] * qc, axis=0, keepdims=True) + sb_ref[h:h + 1, :]
            s_self = jnp.sum(qc * kn, axis=0, keepdims=True) + sb0_ref[h:h + 1, 0:1]
            m = jnp.maximum(jnp.max(sc, axis=1, keepdims=True), s_self)
            pe = jnp.exp(sc - m)
            p_self = jnp.exp(s_self - m)
            den = jnp.sum(pe, axis=1, keepdims=True) + p_self
            num = jnp.sum(vc_ref[bi, rows, :] * pe, axis=1, keepdims=True) + p_self * vn
            at_ref[rows, :] = jnp.where(sel, num / den, at_ref[rows, :])


def _sample_attn_call(qt, kt, vt, sbias, sbias0, ck, cv, layer):
    db, feat, npos = ck.shape[1], ck.shape[2], ck.shape[3]
    bb = SAMPLE_BLOCK
    full = lambda shape: pl.BlockSpec(shape, lambda j: (0,) * len(shape))
    cache = pl.BlockSpec((None, bb, feat, npos), lambda j: (layer, j, 0, 0))
    return pl.pallas_call(
        _sample_attn_kernel,
        grid=(db // bb,),
        in_specs=[full(qt.shape), full(kt.shape), full(vt.shape), full(sbias.shape), full(sbias0.shape),
                  cache, cache],
        out_specs=full((feat, db)),
        out_shape=jax.ShapeDtypeStruct((feat, db), F32),
        compiler_params=_params("arbitrary"),
        name="sample_attn",
    )(qt, kt, vt, sbias, sbias0, ck, cv)


def _sample_out_kernel(at_ref, co_ref, g_ref, x_ref, w_ref, lg_ref, lb_ref, y_ref, *, alpha):
    gated = (at_ref[...].T * g_ref[...]).astype(BF16)
    out = (jnp.dot(co_ref[...].astype(BF16), w_ref[0:CONV_W, :], preferred_element_type=F32)
           + jnp.dot(gated, w_ref[CONV_W:D_MODEL, :], preferred_element_type=F32))
    y_ref[...] = _layer_norm(alpha * x_ref[...] + out, lg_ref[...], lb_ref[...])


def _sample_out_call(at, co, gate, xs, w_out_l, g_l, b_l, alpha):
    return pl.pallas_call(
        functools.partial(_sample_out_kernel, alpha=alpha),
        out_shape=jax.ShapeDtypeStruct(xs.shape, F32),
        compiler_params=pltpu.CompilerParams(vmem_limit_bytes=VMEM_LIMIT),
        name="sample_out",
    )(at, co, gate, xs, w_out_l, g_l, b_l)


def _sample_bias(rel_bias, n_past):
    def table(dist):
        mult = sum(((dist % d == 0) & (dist <= w)).astype(F32) for w, d in DILATIONS)
        b = rel_bias[_rel_bucket(dist)].astype(F32) + jnp.log(jnp.maximum(mult, 1.0))[:, None]
        return jnp.where((mult > 0)[:, None], b, NEG).T
    cache = table(n_past - jnp.arange(n_past))
    own = jnp.broadcast_to(table(jnp.zeros((1,), jnp.int32)), (N_HEADS, LANES))
    return cache, own


def kernel(x_prompt, x_sample, cache_k, cache_v, state_conv, w_in, conv_w, w_out, ln_g, ln_b, rel_bias):
    depth = w_in.shape[0]
    bsz, slen, _ = x_prompt.shape
    db, n_past = cache_k.shape[1], cache_k.shape[2]
    assert x_sample.shape[1] == 1
    alpha = (2 * depth) ** 0.25
    keep = min(MAX_WINDOW, slen)
    w_in_b = w_in.astype(BF16)
    w_out_b = w_out.astype(BF16)
    ln_g2 = ln_g.reshape(depth, 1, D_MODEL)
    ln_b2 = ln_b.reshape(depth, 1, D_MODEL)
    biases = [_prompt_bias(rel_bias, d) for _, d in DILATIONS]
    head = jnp.arange(ATTN_W) // HEAD_DIM
    expand = (jnp.arange(LANES)[:, None] == head[None, :]).astype(BF16)

    from_fp = lambda t, lead: jnp.transpose(t.reshape(lead + (N_HEADS, HEAD_DIM, t.shape[-1])),
                                            tuple(range(len(lead))) + (len(lead) + 2, len(lead), len(lead) + 1))

    hp = x_prompt
    kp_l, vp_l, cp_l = [], [], []
    for l in range(depth):
        q, k, v, gate, co, kt, vt, cs = _proj_call(hp, w_in_b[l], conv_w[l], keep)
        branches = [_attn_call(q, k, v, bias, d) for bias, (_, d) in zip(biases, DILATIONS)]
        hp = _out_call([o for o, _ in branches], [s for _, s in branches], gate, co, hp, expand,
                       w_out_b[l], ln_g2[l], ln_b2[l], alpha)
        kp_l.append(from_fp(kt, (bsz,)))
        vp_l.append(from_fp(vt, (bsz,)))
        cp_l.append(cs)

    ck = jnp.transpose(cache_k, (0, 1, 3, 4, 2)).reshape(depth, db, ATTN_W, n_past)
    cv = jnp.transpose(cache_v, (0, 1, 3, 4, 2)).reshape(depth, db, ATTN_W, n_past)
    sbias, sbias0 = _sample_bias(rel_bias, n_past)
    hs = x_sample.reshape(db, D_MODEL)
    ks_l, vs_l, cs_l = [], [], []
    for l in range(depth):
        s0, s1 = state_conv[l, :, 0, :], state_conv[l, :, 1, :]
        qt, kt, vt, co, gate, u = _sample_proj_call(hs, w_in_b[l], conv_w[l], s0, s1)
        at = _sample_attn_call(qt, kt, vt, sbias, sbias0, ck, cv, l)
        hs = _sample_out_call(at, co, gate, hs, w_out_b[l], ln_g2[l], ln_b2[l], alpha)
        ks_l.append(from_fp(kt, ()).reshape(db, 1, N_HEADS, HEAD_DIM))
        vs_l.append(from_fp(vt, ()).reshape(db, 1, N_HEADS, HEAD_DIM))
        cs_l.append(jnp.stack([s1, u], axis=1))

    return (hp, hs.reshape(db, 1, D_MODEL), jnp.stack(kp_l), jnp.stack(vp_l), jnp.stack(cp_l),
            jnp.stack(ks_l), jnp.stack(vs_l), jnp.stack(cs_l))
```

```python
import functools
import math

import jax
import jax.numpy as jnp
from jax import lax
from jax.experimental import pallas as pl
from jax.experimental.pallas import tpu as pltpu

F32 = jnp.float32
BF16 = jnp.bfloat16

D_MODEL = 1024
HEAD_DIM = 64
N_HEADS = 8
ATTN_W = N_HEADS * HEAD_DIM
CONV_W = D_MODEL - ATTN_W
CONV_K = 3
DILATIONS = ((128, 1), (512, 4), (2048, 16))
MAX_WINDOW = 2048
N_BUCKETS = 32
MAX_DISTANCE = MAX_WINDOW
BLOCK = 128
LN_EPS = 1e-5
SCALE = HEAD_DIM ** -0.5
LOG2E = math.log2(math.e)
NEG = -1e30

LANES = 128
SEQ_TILE = 512
SAMPLE_BLOCK = 2
VMEM_LIMIT = 56 * 1024 * 1024


def _silu(x):
    return x * (1.0 / (1.0 + jnp.exp(-x)))


def _layer_norm(y, g, b):
    mu = jnp.mean(y, axis=-1, keepdims=True)
    yc = y - mu
    var = jnp.mean(yc * yc, axis=-1, keepdims=True)
    return yc * lax.rsqrt(var + LN_EPS) * g + b


def _rel_bucket(dist):
    n_exact = N_BUCKETS // 2
    d = jnp.maximum(dist, 0)
    large = n_exact + (jnp.log(jnp.maximum(d, 1).astype(F32) / n_exact)
                       / math.log(MAX_DISTANCE / n_exact) * (N_BUCKETS - n_exact)).astype(jnp.int32)
    large = jnp.minimum(large, N_BUCKETS - 1)
    return jnp.where(d < n_exact, d, large)


def _params(*sem):
    return pltpu.CompilerParams(dimension_semantics=sem, vmem_limit_bytes=VMEM_LIMIT)


def _proj_kernel(x_ref, w_ref, cw_ref, q_ref, k_ref, v_ref, g_ref, co_ref, kt_ref, vt_ref, cs_ref, ubuf,
                 *, first_kept):
    ts = x_ref.shape[0]
    s = pl.program_id(1)
    xb = x_ref[...].astype(BF16)

    def col(j):
        return jnp.dot(xb, w_ref[:, j * 512:(j + 1) * 512], preferred_element_type=F32)

    u = col(2) * col(0)

    @pl.when(s == 0)
    def _():
        ubuf[0:8, :] = jnp.zeros((8, CONV_W), F32)

    ubuf[8:8 + ts, :] = u
    cw = cw_ref[...]
    cy = cw[0:1, :] * ubuf[6:6 + ts, :] + cw[1:2, :] * ubuf[7:7 + ts, :] + cw[2:3, :] * u
    co_ref[...] = (col(1) * cy * _silu(col(3))).astype(BF16)
    cs_ref[...] = u[ts - 2:ts, :]
    ubuf[0:8, :] = u[ts - 8:ts, :]

    q_ref[...] = col(4) * (SCALE * LOG2E)
    k = col(5)
    k_ref[...] = k
    v = col(6)
    v_ref[...] = v
    g_ref[...] = _silu(col(7)).astype(BF16)

    @pl.when(s >= first_kept)
    def _():
        kt_ref[...] = k.T
        vt_ref[...] = v.T


def _proj_call(x, w_in_l, conv_w_l, keep):
    bsz, slen, _ = x.shape
    ts = SEQ_TILE
    ns = slen // ts
    off = (slen - keep) // ts
    tile = lambda width: pl.BlockSpec((None, ts, width), lambda b, s: (b, s, 0))
    kept = pl.BlockSpec((None, ATTN_W, ts), lambda b, s: (b, 0, jnp.maximum(s - off, 0)))
    bf = jax.ShapeDtypeStruct((bsz, slen, ATTN_W), BF16)
    fl = jax.ShapeDtypeStruct((bsz, slen, ATTN_W), F32)
    return pl.pallas_call(
        functools.partial(_proj_kernel, first_kept=off),
        grid=(bsz, ns),
        in_specs=[tile(D_MODEL),
                  pl.BlockSpec((D_MODEL, 4 * CONV_W + 4 * ATTN_W), lambda b, s: (0, 0)),
                  pl.BlockSpec((CONV_K, CONV_W), lambda b, s: (0, 0))],
        out_specs=[tile(ATTN_W), tile(ATTN_W), tile(ATTN_W), tile(ATTN_W), tile(CONV_W), kept, kept,
                   pl.BlockSpec((None, CONV_K - 1, CONV_W), lambda b, s: (b, 0, 0))],
        out_shape=[fl, fl, fl, bf, jax.ShapeDtypeStruct((bsz, slen, CONV_W), BF16),
                   jax.ShapeDtypeStruct((bsz, ATTN_W, keep), F32),
                   jax.ShapeDtypeStruct((bsz, ATTN_W, keep), F32),
                   jax.ShapeDtypeStruct((bsz, CONV_K - 1, CONV_W), F32)],
        scratch_shapes=[pltpu.VMEM((ts + 8, CONV_W), F32)],
        compiler_params=_params("arbitrary", "arbitrary"),
        name="prompt_proj",
    )(x, w_in_l, conv_w_l)


NT_DIMS = (((1,), (1,)), ((), ()))
BLOCKS_IN_FLIGHT = 8


def _pair_attention(q, k, v, bias, low):
    qb = q.astype(BF16)
    zero = jnp.zeros_like(qb)
    qm = jnp.concatenate([jnp.where(low, qb, zero), jnp.where(low, zero, qb)], axis=0)
    vb = jnp.concatenate([v.astype(BF16), jnp.ones((2 * BLOCK, LANES), BF16)], axis=1)
    sc = lax.dot_general(qm, k.astype(BF16), NT_DIMS, preferred_element_type=F32) + bias
    m = jnp.max(sc, axis=1, keepdims=True)
    out = jnp.dot(jnp.exp2(sc - m).astype(BF16), vb, preferred_element_type=F32)
    per_head = lambda t: jnp.where(low, t[0:BLOCK], t[BLOCK:2 * BLOCK])
    return (per_head(out[:, 0:LANES]), per_head(jnp.broadcast_to(m, (2 * BLOCK, LANES))),
            per_head(out[:, LANES:2 * LANES]))


def _merge(a, b):
    m = jnp.maximum(a[1], b[1])
    e_a = jnp.exp2(a[1] - m)
    e_b = jnp.exp2(b[1] - m)
    return a[0] * e_a + b[0] * e_b, m, a[2] * e_a + b[2] * e_b


def _attn_kernel(q_ref, k_ref, v_ref, bias_ref, o_ref, r4, acc1, acc4):
    slen = q_ref.shape[0]
    n4 = slen // 4
    nb4 = n4 // BLOCK
    nb16 = slen // 16 // BLOCK
    low = lax.broadcasted_iota(jnp.int32, (BLOCK, LANES), 1) < HEAD_DIM

    def load(acc, rows):
        return acc[0, rows, :], acc[1, rows, :], acc[2, rows, :]

    def store(acc, rows, part):
        for a in range(3):
            acc[a, rows, :] = part[a]

    def regroup(i, carry):
        c, jj = i // nb4, i % nb4
        src = pl.ds(c + 4 * BLOCK * jj, BLOCK, stride=4)
        dst = pl.ds(pl.multiple_of(c * n4 + jj * BLOCK, BLOCK), BLOCK)
        r4[0, dst, :] = q_ref[src, :]
        r4[1, dst, :] = k_ref[src, :]
        r4[2, dst, :] = v_ref[src, :]
        return carry

    lax.fori_loop(0, 4 * nb4, regroup, 0)

    def branch4(i, carry):
        c, j = i // nb4, i % nb4
        t = jnp.minimum(j, 1)
        qs = pl.ds(pl.multiple_of(c * n4 + j * BLOCK, BLOCK), BLOCK)
        ks = pl.ds(pl.multiple_of(c * n4 + jnp.maximum(j - 1, 0) * BLOCK, BLOCK), 2 * BLOCK)
        store(acc4, qs, _pair_attention(r4[0, qs, :], r4[1, ks, :], r4[2, ks, :], bias_ref[1, t], low))
        return carry

    lax.fori_loop(0, 4 * nb4, branch4, 0, unroll=BLOCKS_IN_FLIGHT)

    def branch16(i, carry):
        res, j = i // nb16, i % nb16
        base = (res % 4) * n4 + res // 4
        t = jnp.minimum(j, 1)
        qs = pl.ds(base + 4 * BLOCK * j, BLOCK, stride=4)
        ks = pl.ds(base + 4 * BLOCK * jnp.maximum(j - 1, 0), 2 * BLOCK, stride=4)
        part = _pair_attention(r4[0, qs, :], r4[1, ks, :], r4[2, ks, :], bias_ref[2, t], low)
        store(acc4, qs, _merge(load(acc4, qs), part))
        return carry

    lax.fori_loop(0, 16 * nb16, branch16, 0, unroll=BLOCKS_IN_FLIGHT)

    def branch1(j, carry):
        t = jnp.minimum(j, 1)
        qs = pl.ds(pl.multiple_of(j * BLOCK, BLOCK), BLOCK)
        ks = pl.ds(pl.multiple_of(jnp.maximum(j - 1, 0) * BLOCK, BLOCK), 2 * BLOCK)
        store(acc1, qs, _pair_attention(q_ref[qs, :], k_ref[ks, :], v_ref[ks, :], bias_ref[0, t], low))
        return carry

    lax.fori_loop(0, slen // BLOCK, branch1, 0, unroll=BLOCKS_IN_FLIGHT)

    def mix(i, carry):
        c, jj = i // nb4, i % nb4
        rows4 = pl.ds(pl.multiple_of(c * n4 + jj * BLOCK, BLOCK), BLOCK)
        rows1 = pl.ds(c + 4 * BLOCK * jj, BLOCK, stride=4)
        num, _, den = _merge(load(acc4, rows4), load(acc1, rows1))
        o_ref[rows1, :] = num / den
        return carry

    lax.fori_loop(0, 4 * nb4, mix, 0)


def _attn_call(q, k, v, bias):
    bsz, slen, _ = q.shape
    assert slen % (16 * 2 * BLOCK) == 0
    n_pairs = ATTN_W // LANES
    blk = pl.BlockSpec((None, slen, LANES), lambda b, p: (b, 0, p))
    return pl.pallas_call(
        _attn_kernel,
        grid=(bsz, n_pairs),
        in_specs=[blk, blk, blk,
                  pl.BlockSpec((len(DILATIONS), 2, 2 * BLOCK, 2 * BLOCK), lambda b, p: (0, 0, p, 0))],
        out_specs=blk,
        out_shape=jax.ShapeDtypeStruct((bsz, slen, ATTN_W), F32),
        scratch_shapes=[pltpu.VMEM((3, slen, LANES), F32), pltpu.VMEM((3, slen, LANES), F32),
                        pltpu.VMEM((3, slen, LANES), F32)],
        compiler_params=_params("arbitrary", "arbitrary"),
        name="prompt_attn",
    )(q, k, v, bias)


def _bias_lookup(rel_bias, dist):
    onehot = (_rel_bucket(dist)[..., None] == jnp.arange(N_BUCKETS)).astype(F32)
    return jnp.einsum('...i,ih->...h', onehot, rel_bias.astype(F32), precision=lax.Precision.HIGHEST)


def _prompt_bias(rel_bias):
    steps = BLOCK
    qi = jnp.arange(BLOCK)[:, None]
    kj = jnp.arange(2 * BLOCK)[None, :]
    step1 = qi + BLOCK - kj
    ok1 = (step1 >= 0) & (step1 <= steps)
    step0 = qi - kj
    ok0 = (step0 >= 0) & (kj < BLOCK)

    def table(step, ok, dil):
        b = _bias_lookup(rel_bias, jnp.clip(step, 0, steps) * dil)
        b = jnp.transpose(jnp.where(ok[:, :, None], b * LOG2E, NEG), (2, 0, 1))
        return b.reshape(N_HEADS * BLOCK, 2 * BLOCK)

    return jnp.stack([jnp.stack([table(step0, ok0, d), table(step1, ok1, d)]) for _, d in DILATIONS])


def _out_kernel(att_ref, g_ref, co_ref, x_ref, w_ref, lg_ref, lb_ref, y_ref, *, alpha):
    gated = (att_ref[...] * g_ref[...].astype(F32)).astype(BF16)
    out = (jnp.dot(co_ref[...], w_ref[0:CONV_W, :], preferred_element_type=F32)
           + jnp.dot(gated, w_ref[CONV_W:D_MODEL, :], preferred_element_type=F32))
    y_ref[...] = _layer_norm(alpha * x_ref[...] + out, lg_ref[...], lb_ref[...])


def _out_call(att, gate, co, x, w_out_l, g_l, b_l, alpha):
    bsz, slen, _ = x.shape
    ts = SEQ_TILE
    tile = lambda width: pl.BlockSpec((None, ts, width), lambda b, s: (b, s, 0))
    full = lambda shape: pl.BlockSpec(shape, lambda b, s: (0,) * len(shape))
    return pl.pallas_call(
        functools.partial(_out_kernel, alpha=alpha),
        grid=(bsz, slen // ts),
        in_specs=[tile(ATTN_W), tile(ATTN_W), tile(CONV_W), tile(D_MODEL),
                  full((D_MODEL, D_MODEL)), full((1, D_MODEL)), full((1, D_MODEL))],
        out_specs=tile(D_MODEL),
        out_shape=jax.ShapeDtypeStruct((bsz, slen, D_MODEL), F32),
        compiler_params=_params("arbitrary", "arbitrary"),
        name="prompt_out",
    )(att, gate, co, x, w_out_l, g_l, b_l)


def _sample_proj_kernel(x_ref, w_ref, cw_ref, s0_ref, s1_ref, qt_ref, kt_ref, vt_ref, co_ref, g_ref, cs_ref):
    xb = x_ref[...].astype(BF16)

    def col(c):
        return jnp.dot(xb, w_ref[:, c * 512:(c + 1) * 512], preferred_element_type=F32)

    u = col(2) * col(0)
    cw = cw_ref[...]
    cy = cw[0:1, :] * s0_ref[...] + cw[1:2, :] * s1_ref[...] + cw[2:3, :] * u
    co_ref[...] = col(1) * cy * _silu(col(3))
    cs_ref[...] = u
    qt_ref[...] = (col(4) * SCALE).T
    kt_ref[...] = col(5).T
    vt_ref[...] = col(6).T
    g_ref[...] = _silu(col(7))


def _sample_proj_call(xs, w_in_l, conv_w_l, s0, s1):
    db = xs.shape[0]
    tr = jax.ShapeDtypeStruct((ATTN_W, db), F32)
    row = jax.ShapeDtypeStruct((db, ATTN_W), F32)
    return pl.pallas_call(
        _sample_proj_kernel,
        out_shape=[tr, tr, tr, row, row, row],
        compiler_params=pltpu.CompilerParams(vmem_limit_bytes=VMEM_LIMIT),
        name="sample_proj",
    )(xs, w_in_l, conv_w_l, s0, s1)


def _sample_attn_kernel(qt_ref, kt_ref, vt_ref, sb_ref, sb0_ref, kc_ref, vc_ref, at_ref):
    j = pl.program_id(0)
    bb = kc_ref.shape[0]
    db = qt_ref.shape[1]
    lane = lax.broadcasted_iota(jnp.int32, (HEAD_DIM, db), 1)

    @pl.when(j == 0)
    def _():
        at_ref[...] = jnp.zeros(at_ref.shape, F32)

    for bi in range(bb):
        sel = lane == j * bb + bi
        for h in range(N_HEADS):
            rows = slice(h * HEAD_DIM, (h + 1) * HEAD_DIM)
            pick = lambda ref: jnp.sum(jnp.where(sel, ref[rows, :], 0.0), axis=1, keepdims=True)
            qc, kn, vn = pick(qt_ref), pick(kt_ref), pick(vt_ref)
            sc = jnp.sum(kc_ref[bi, rows, :] * qc, axis=0, keepdims=True) + sb_ref[h:h + 1, :]
            s_self = jnp.sum(qc * kn, axis=0, keepdims=True) + sb0_ref[h:h + 1, 0:1]
            m = jnp.maximum(jnp.max(sc, axis=1, keepdims=True), s_self)
            pe = jnp.exp(sc - m)
            p_self = jnp.exp(s_self - m)
            den = jnp.sum(pe, axis=1, keepdims=True) + p_self
            num = jnp.sum(vc_ref[bi, rows, :] * pe, axis=1, keepdims=True) + p_self * vn
            at_ref[rows, :] = jnp.where(sel, num / den, at_ref[rows, :])


def _sample_attn_call(qt, kt, vt, sbias, sbias0, ck, cv, layer):
    db, feat, npos = ck.shape[1], ck.shape[2], ck.shape[3]
    bb = SAMPLE_BLOCK
    full = lambda shape: pl.BlockSpec(shape, lambda j: (0,) * len(shape))
    cache = pl.BlockSpec((None, bb, feat, npos), lambda j: (layer, j, 0, 0))
    return pl.pallas_call(
        _sample_attn_kernel,
        grid=(db // bb,),
        in_specs=[full(qt.shape), full(kt.shape), full(vt.shape), full(sbias.shape), full(sbias0.shape),
                  cache, cache],
        out_specs=full((feat, db)),
        out_shape=jax.ShapeDtypeStruct((feat, db), F32),
        compiler_params=_params("arbitrary"),
        name="sample_attn",
    )(qt, kt, vt, sbias, sbias0, ck, cv)


def _sample_out_kernel(at_ref, co_ref, g_ref, x_ref, w_ref, lg_ref, lb_ref, y_ref, *, alpha):
    gated = (at_ref[...].T * g_ref[...]).astype(BF16)
    out = (jnp.dot(co_ref[...].astype(BF16), w_ref[0:CONV_W, :], preferred_element_type=F32)
           + jnp.dot(gated, w_ref[CONV_W:D_MODEL, :], preferred_element_type=F32))
    y_ref[...] = _layer_norm(alpha * x_ref[...] + out, lg_ref[...], lb_ref[...])


def _sample_out_call(at, co, gate, xs, w_out_l, g_l, b_l, alpha):
    return pl.pallas_call(
        functools.partial(_sample_out_kernel, alpha=alpha),
        out_shape=jax.ShapeDtypeStruct(xs.shape, F32),
        compiler_params=pltpu.CompilerParams(vmem_limit_bytes=VMEM_LIMIT),
        name="sample_out",
    )(at, co, gate, xs, w_out_l, g_l, b_l)


def _sample_bias(rel_bias, n_past):
    def table(dist):
        mult = sum(((dist % d == 0) & (dist <= w)).astype(F32) for w, d in DILATIONS)
        b = _bias_lookup(rel_bias, dist) + jnp.log(jnp.maximum(mult, 1.0))[:, None]
        return jnp.where((mult > 0)[:, None], b, NEG).T
    cache = table(n_past - jnp.arange(n_past))
    own = jnp.broadcast_to(table(jnp.zeros((1,), jnp.int32)), (N_HEADS, LANES))
    return cache, own


def kernel(x_prompt, x_sample, cache_k, cache_v, state_conv, w_in, conv_w, w_out, ln_g, ln_b, rel_bias):
    depth = w_in.shape[0]
    bsz, slen, _ = x_prompt.shape
    db, n_past = cache_k.shape[1], cache_k.shape[2]
    assert x_sample.shape[1] == 1
    alpha = (2 * depth) ** 0.25
    keep = min(MAX_WINDOW, slen)
    w_in_b = w_in.astype(BF16)
    w_out_b = w_out.astype(BF16)
    ln_g2 = ln_g.reshape(depth, 1, D_MODEL)
    ln_b2 = ln_b.reshape(depth, 1, D_MODEL)
    bias = _prompt_bias(rel_bias)

    from_fp = lambda t, lead: jnp.transpose(t.reshape(lead + (N_HEADS, HEAD_DIM, t.shape[-1])),
                                            tuple(range(len(lead))) + (len(lead) + 2, len(lead), len(lead) + 1))

    hp = x_prompt
    kp_l, vp_l, cp_l = [], [], []
    for l in range(depth):
        q, k, v, gate, co, kt, vt, cs = _proj_call(hp, w_in_b[l], conv_w[l], keep)
        att = _attn_call(q, k, v, bias)
        hp = _out_call(att, gate, co, hp, w_out_b[l], ln_g2[l], ln_b2[l], alpha)
        kp_l.append(from_fp(kt, (bsz,)))
        vp_l.append(from_fp(vt, (bsz,)))
        cp_l.append(cs)

    ck = jnp.transpose(cache_k, (0, 1, 3, 4, 2)).reshape(depth, db, ATTN_W, n_past)
    cv = jnp.transpose(cache_v, (0, 1, 3, 4, 2)).reshape(depth, db, ATTN_W, n_past)
    sbias, sbias0 = _sample_bias(rel_bias, n_past)
    hs = x_sample.reshape(db, D_MODEL)
    ks_l, vs_l, cs_l = [], [], []
    for l in range(depth):
        s0, s1 = state_conv[l, :, 0, :], state_conv[l, :, 1, :]
        qt, kt, vt, co, gate, u = _sample_proj_call(hs, w_in_b[l], conv_w[l], s0, s1)
        at = _sample_attn_call(qt, kt, vt, sbias, sbias0, ck, cv, l)
        hs = _sample_out_call(at, co, gate, hs, w_out_b[l], ln_g2[l], ln_b2[l], alpha)
        ks_l.append(from_fp(kt, ()).reshape(db, 1, N_HEADS, HEAD_DIM))
        vs_l.append(from_fp(vt, ()).reshape(db, 1, N_HEADS, HEAD_DIM))
        cs_l.append(jnp.stack([s1, u], axis=1))

    return (hp, hs.reshape(db, 1, D_MODEL), jnp.stack(kp_l), jnp.stack(vp_l), jnp.stack(cp_l),
            jnp.stack(ks_l), jnp.stack(vs_l), jnp.stack(cs_l))
```

```python
import functools
import math
from typing import NamedTuple

import jax
import jax.numpy as jnp
import numpy as np
from jax import lax
from jax.experimental import pallas as pl
from jax.experimental.pallas import tpu as pltpu

F32 = jnp.float32
BF16 = jnp.bfloat16

D_MODEL = 1024
HEAD_DIM = 64
N_HEADS = 8
ATTN_W = N_HEADS * HEAD_DIM
CONV_W = D_MODEL - ATTN_W
CONV_K = 3
DILATIONS = ((128, 1), (512, 4), (2048, 16))
MAX_WINDOW = 2048
N_BUCKETS = 32
MAX_DISTANCE = MAX_WINDOW
BLOCK = 128
LN_EPS = 1e-5
SCALE = HEAD_DIM ** -0.5
LOG2E = math.log2(math.e)
NEG = -1e30

LANES = 128
SEQ_TILE = 256
VMEM_LIMIT = 56 * 1024 * 1024


def _silu(x):
    return x * (1.0 / (1.0 + jnp.exp(-x)))


def _layer_norm(y, g, b):
    mu = jnp.mean(y, axis=-1, keepdims=True)
    yc = y - mu
    var = jnp.mean(yc * yc, axis=-1, keepdims=True)
    return yc * lax.rsqrt(var + LN_EPS) * g + b


def _rel_bucket(dist):
    n_exact = N_BUCKETS // 2
    d = np.maximum(dist, 0)
    large = n_exact + (np.log(np.maximum(d, 1).astype(np.float32) / np.float32(n_exact))
                       / np.float32(math.log(MAX_DISTANCE / n_exact)) * np.float32(N_BUCKETS - n_exact)
                       ).astype(np.int32)
    large = np.minimum(large, N_BUCKETS - 1)
    return np.where(d < n_exact, d, large)


def _params(*sem):
    return pltpu.CompilerParams(dimension_semantics=sem, vmem_limit_bytes=VMEM_LIMIT)


def _proj_kernel(x_ref, w_ref, cw_ref, *rest, first_kept, rider):
    ride_refs, (q_ref, k_ref, v_ref, g_ref, co_ref, kt_ref, vt_ref, cs_ref, at_ref, ubuf) = rest[:7], rest[7:]
    _ride(ride_refs, at_ref, rider)
    ts = x_ref.shape[0]
    s = pl.program_id(1)
    xb = x_ref[...].astype(BF16)

    def col(j):
        return jnp.dot(xb, w_ref[:, j * 512:(j + 1) * 512], preferred_element_type=F32)

    u = col(2) * col(0)

    @pl.when(s == 0)
    def _():
        ubuf[0:8, :] = jnp.zeros((8, CONV_W), F32)

    ubuf[8:8 + ts, :] = u
    cw = cw_ref[...]
    cy = cw[0:1, :] * ubuf[6:6 + ts, :] + cw[1:2, :] * ubuf[7:7 + ts, :] + cw[2:3, :] * u
    co_ref[...] = (col(1) * cy * _silu(col(3))).astype(BF16)
    cs_ref[...] = u[ts - 2:ts, :]
    ubuf[0:8, :] = u[ts - 8:ts, :]

    q_ref[...] = col(4) * (SCALE * LOG2E)
    k = col(5)
    k_ref[...] = k
    v = col(6)
    v_ref[...] = v
    g_ref[...] = _silu(col(7)).astype(BF16)

    @pl.when(s >= first_kept)
    def _():
        kt_ref[...] = k.T
        vt_ref[...] = v.T


def _proj_call(x, w_in_l, conv_w_l, keep, ride_ops, rider):
    bsz, slen, _ = x.shape
    ts = SEQ_TILE
    ns = slen // ts
    off = (slen - keep) // ts
    ride_in, ride_out, ride_shape = _ride_specs(ride_ops, rider, ns)
    tile = lambda width: pl.BlockSpec((None, ts, width), lambda b, s: (b, s, 0))
    kept = pl.BlockSpec((None, ATTN_W, ts), lambda b, s: (b, 0, jnp.maximum(s - off, 0)))
    bf = jax.ShapeDtypeStruct((bsz, slen, ATTN_W), BF16)
    fl = jax.ShapeDtypeStruct((bsz, slen, ATTN_W), F32)
    return pl.pallas_call(
        functools.partial(_proj_kernel, first_kept=off, rider=rider),
        grid=(bsz, ns),
        in_specs=[tile(D_MODEL),
                  pl.BlockSpec((D_MODEL, 4 * CONV_W + 4 * ATTN_W), lambda b, s: (0, 0)),
                  pl.BlockSpec((CONV_K, CONV_W), lambda b, s: (0, 0))] + ride_in,
        out_specs=[tile(ATTN_W), tile(ATTN_W), tile(ATTN_W), tile(ATTN_W), tile(CONV_W), kept, kept,
                   pl.BlockSpec((None, CONV_K - 1, CONV_W), lambda b, s: (b, 0, 0)), ride_out],
        out_shape=[fl, fl, fl, bf, jax.ShapeDtypeStruct((bsz, slen, CONV_W), BF16),
                   jax.ShapeDtypeStruct((bsz, ATTN_W, keep), F32),
                   jax.ShapeDtypeStruct((bsz, ATTN_W, keep), F32),
                   jax.ShapeDtypeStruct((bsz, CONV_K - 1, CONV_W), F32), ride_shape],
        scratch_shapes=[pltpu.VMEM((ts + 8, CONV_W), F32)],
        compiler_params=_params("arbitrary", "arbitrary"),
        name="prompt_proj",
    )(x, w_in_l, conv_w_l, *ride_ops)


NT_DIMS = (((1,), (1,)), ((), ()))
BLOCKS_IN_FLIGHT = 8


def _pair_attention(q, k, v, bias, low):
    qb = q.astype(BF16)
    zero = jnp.zeros_like(qb)
    qm = jnp.concatenate([jnp.where(low, qb, zero), jnp.where(low, zero, qb)], axis=0)
    vb = jnp.concatenate([v.astype(BF16), jnp.ones((2 * BLOCK, LANES), BF16)], axis=1)
    sc = lax.dot_general(qm, k.astype(BF16), NT_DIMS, preferred_element_type=F32) + bias
    m = jnp.max(sc, axis=1, keepdims=True)
    out = jnp.dot(jnp.exp2(sc - m).astype(BF16), vb, preferred_element_type=F32)
    per_head = lambda t: jnp.where(low, t[0:BLOCK], t[BLOCK:2 * BLOCK])
    return (per_head(out[:, 0:LANES]), per_head(jnp.broadcast_to(m, (2 * BLOCK, LANES))),
            per_head(out[:, LANES:2 * LANES]))


def _merge(a, b):
    m = jnp.maximum(a[1], b[1])
    e_a = jnp.exp2(a[1] - m)
    e_b = jnp.exp2(b[1] - m)
    return a[0] * e_a + b[0] * e_b, m, a[2] * e_a + b[2] * e_b


def _attn_kernel(q_ref, k_ref, v_ref, bias_ref, o_ref, r4, acc1, acc4):
    slen = q_ref.shape[0]
    n4 = slen // 4
    nb4 = n4 // BLOCK
    nb16 = slen // 16 // BLOCK
    low = lax.broadcasted_iota(jnp.int32, (BLOCK, LANES), 1) < HEAD_DIM

    def load(acc, rows):
        return acc[0, rows, :], acc[1, rows, :], acc[2, rows, :]

    def store(acc, rows, part):
        for a in range(3):
            acc[a, rows, :] = part[a]

    def regroup(i, carry):
        c, jj = i // nb4, i % nb4
        src = pl.ds(c + 4 * BLOCK * jj, BLOCK, stride=4)
        dst = pl.ds(pl.multiple_of(c * n4 + jj * BLOCK, BLOCK), BLOCK)
        r4[0, dst, :] = q_ref[src, :]
        r4[1, dst, :] = k_ref[src, :]
        r4[2, dst, :] = v_ref[src, :]
        return carry

    lax.fori_loop(0, 4 * nb4, regroup, 0)

    def branch4(i, carry):
        c, j = i // nb4, i % nb4
        t = jnp.minimum(j, 1)
        qs = pl.ds(pl.multiple_of(c * n4 + j * BLOCK, BLOCK), BLOCK)
        ks = pl.ds(pl.multiple_of(c * n4 + jnp.maximum(j - 1, 0) * BLOCK, BLOCK), 2 * BLOCK)
        store(acc4, qs, _pair_attention(r4[0, qs, :], r4[1, ks, :], r4[2, ks, :], bias_ref[1, t], low))
        return carry

    lax.fori_loop(0, 4 * nb4, branch4, 0, unroll=BLOCKS_IN_FLIGHT)

    def branch16(i, carry):
        res, j = i // nb16, i % nb16
        base = (res % 4) * n4 + res // 4
        t = jnp.minimum(j, 1)
        qs = pl.ds(base + 4 * BLOCK * j, BLOCK, stride=4)
        ks = pl.ds(base + 4 * BLOCK * jnp.maximum(j - 1, 0), 2 * BLOCK, stride=4)
        part = _pair_attention(r4[0, qs, :], r4[1, ks, :], r4[2, ks, :], bias_ref[2, t], low)
        store(acc4, qs, _merge(load(acc4, qs), part))
        return carry

    lax.fori_loop(0, 16 * nb16, branch16, 0, unroll=BLOCKS_IN_FLIGHT)

    def branch1(j, carry):
        t = jnp.minimum(j, 1)
        qs = pl.ds(pl.multiple_of(j * BLOCK, BLOCK), BLOCK)
        ks = pl.ds(pl.multiple_of(jnp.maximum(j - 1, 0) * BLOCK, BLOCK), 2 * BLOCK)
        store(acc1, qs, _pair_attention(q_ref[qs, :], k_ref[ks, :], v_ref[ks, :], bias_ref[0, t], low))
        return carry

    lax.fori_loop(0, slen // BLOCK, branch1, 0, unroll=BLOCKS_IN_FLIGHT)

    def mix(i, carry):
        c, jj = i // nb4, i % nb4
        rows4 = pl.ds(pl.multiple_of(c * n4 + jj * BLOCK, BLOCK), BLOCK)
        rows1 = pl.ds(c + 4 * BLOCK * jj, BLOCK, stride=4)
        num, _, den = _merge(load(acc4, rows4), load(acc1, rows1))
        o_ref[rows1, :] = num / den
        return carry

    lax.fori_loop(0, 4 * nb4, mix, 0)


def _attn_call(q, k, v, bias):
    bsz, slen, _ = q.shape
    assert slen % (16 * 2 * BLOCK) == 0
    n_pairs = ATTN_W // LANES
    blk = pl.BlockSpec((None, slen, LANES), lambda b, p: (b, 0, p))
    return pl.pallas_call(
        _attn_kernel,
        grid=(bsz, n_pairs),
        in_specs=[blk, blk, blk,
                  pl.BlockSpec((len(DILATIONS), 2, 2 * BLOCK, 2 * BLOCK), lambda b, p: (0, 0, p, 0))],
        out_specs=blk,
        out_shape=jax.ShapeDtypeStruct((bsz, slen, ATTN_W), F32),
        scratch_shapes=[pltpu.VMEM((3, slen, LANES), F32), pltpu.VMEM((3, slen, LANES), F32),
                        pltpu.VMEM((3, slen, LANES), F32)],
        compiler_params=_params("arbitrary", "arbitrary"),
        name="prompt_attn",
    )(q, k, v, bias)


def _bias_lookup(rel_bias, dist):
    onehot = (_rel_bucket(dist)[..., None] == np.arange(N_BUCKETS)).astype(np.float32)
    return jnp.einsum('...i,ih->...h', onehot, rel_bias.astype(F32), precision=lax.Precision.HIGHEST)


def _prompt_bias(rel_bias):
    steps = BLOCK
    qi = np.arange(BLOCK)[:, None]
    kj = np.arange(2 * BLOCK)[None, :]
    step1 = qi + BLOCK - kj
    ok1 = (step1 >= 0) & (step1 <= steps)
    step0 = qi - kj
    ok0 = (step0 >= 0) & (kj < BLOCK)

    def table(step, ok, dil):
        b = _bias_lookup(rel_bias, np.clip(step, 0, steps) * dil)
        b = jnp.transpose(jnp.where(ok[:, :, None], b * LOG2E, NEG), (2, 0, 1))
        return b.reshape(N_HEADS * BLOCK, 2 * BLOCK)

    return jnp.stack([jnp.stack([table(step0, ok0, d), table(step1, ok1, d)]) for _, d in DILATIONS])


def _out_kernel(att_ref, g_ref, co_ref, x_ref, w_ref, lg_ref, lb_ref, *rest, alpha, rider):
    ride_refs, (y_ref, at_ref) = rest[:7], rest[7:]
    _ride(ride_refs, at_ref, rider)
    gated = (att_ref[...] * g_ref[...].astype(F32)).astype(BF16)
    out = (jnp.dot(co_ref[...], w_ref[0:CONV_W, :], preferred_element_type=F32)
           + jnp.dot(gated, w_ref[CONV_W:D_MODEL, :], preferred_element_type=F32))
    y_ref[...] = _layer_norm(alpha * x_ref[...] + out, lg_ref[...], lb_ref[...])


def _out_call(att, gate, co, x, w_out_l, g_l, b_l, alpha, ride_ops, rider):
    bsz, slen, _ = x.shape
    ts = SEQ_TILE
    ride_in, ride_out, ride_shape = _ride_specs(ride_ops, rider, slen // ts)
    tile = lambda width: pl.BlockSpec((None, ts, width), lambda b, s: (b, s, 0))
    full = lambda shape: pl.BlockSpec(shape, lambda b, s: (0,) * len(shape))
    return pl.pallas_call(
        functools.partial(_out_kernel, alpha=alpha, rider=rider),
        grid=(bsz, slen // ts),
        in_specs=[tile(ATTN_W), tile(ATTN_W), tile(CONV_W), tile(D_MODEL),
                  full((D_MODEL, D_MODEL)), full((1, D_MODEL)), full((1, D_MODEL))] + ride_in,
        out_specs=[tile(D_MODEL), ride_out],
        out_shape=[jax.ShapeDtypeStruct((bsz, slen, D_MODEL), F32), ride_shape],
        compiler_params=_params("arbitrary", "arbitrary"),
        name="prompt_out",
    )(att, gate, co, x, w_out_l, g_l, b_l, *ride_ops)


def _sample_proj_kernel(x_ref, w_ref, cw_ref, s0_ref, s1_ref, qt_ref, kt_ref, vt_ref, co_ref, g_ref, cs_ref):
    xb = x_ref[...].astype(BF16)

    def col(c):
        return jnp.dot(xb, w_ref[:, c * 512:(c + 1) * 512], preferred_element_type=F32)

    u = col(2) * col(0)
    cw = cw_ref[...]
    cy = cw[0:1, :] * s0_ref[...] + cw[1:2, :] * s1_ref[...] + cw[2:3, :] * u
    co_ref[...] = col(1) * cy * _silu(col(3))
    cs_ref[...] = u
    qt_ref[...] = (col(4) * SCALE).T
    kt_ref[...] = col(5).T
    vt_ref[...] = col(6).T
    g_ref[...] = _silu(col(7))


def _sample_proj_call(xs, w_in_l, conv_w_l, s0, s1):
    db = xs.shape[0]
    tr = jax.ShapeDtypeStruct((ATTN_W, db), F32)
    row = jax.ShapeDtypeStruct((db, ATTN_W), F32)
    return pl.pallas_call(
        _sample_proj_kernel,
        out_shape=[tr, tr, tr, row, row, row],
        compiler_params=pltpu.CompilerParams(vmem_limit_bytes=VMEM_LIMIT),
        name="sample_proj",
    )(xs, w_in_l, conv_w_l, s0, s1)


def _sample_attend(elem, qt_ref, kt_ref, vt_ref, sb_ref, sb0_ref, kc_ref, vc_ref, at_ref):
    db = qt_ref.shape[1]
    sel = lax.broadcasted_iota(jnp.int32, (HEAD_DIM, db), 1) == elem
    for h in range(N_HEADS):
        rows = slice(h * HEAD_DIM, (h + 1) * HEAD_DIM)
        pick = lambda ref: jnp.sum(jnp.where(sel, ref[rows, :], 0.0), axis=1, keepdims=True)
        qc, kn, vn = pick(qt_ref), pick(kt_ref), pick(vt_ref)
        sc = jnp.sum(kc_ref[rows, :] * qc, axis=0, keepdims=True) + sb_ref[h:h + 1, :]
        s_self = jnp.sum(qc * kn, axis=0, keepdims=True) + sb0_ref[h:h + 1, 0:1]
        m = jnp.maximum(jnp.max(sc, axis=1, keepdims=True), s_self)
        pe = jnp.exp(sc - m)
        p_self = jnp.exp(s_self - m)
        den = jnp.sum(pe, axis=1, keepdims=True) + p_self
        num = jnp.sum(vc_ref[rows, :] * pe, axis=1, keepdims=True) + p_self * vn
        at_ref[rows, :] = jnp.where(sel, num / den, at_ref[rows, :])


class _Rider(NamedTuple):
    layer: int
    first: int
    per_step: int


def _ride(ride_refs, at_ref, rider):
    qt_ref, kt_ref, vt_ref, sb_ref, sb0_ref, kc_ref, vc_ref = ride_refs
    step = pl.program_id(0) * pl.num_programs(1) + pl.program_id(1)

    @pl.when(step == 0)
    def _():
        at_ref[...] = jnp.zeros(at_ref.shape, F32)

    for bi in range(rider.per_step):
        _sample_attend(rider.first + step * rider.per_step + bi, qt_ref, kt_ref, vt_ref, sb_ref, sb0_ref,
                       kc_ref.at[bi], vc_ref.at[bi], at_ref)


def _ride_specs(ride_ops, rider, ns):
    ck = ride_ops[5]
    db, feat, npos = ck.shape[1], ck.shape[2], ck.shape[3]
    assert rider.first % rider.per_step == 0
    first_block = rider.first // rider.per_step
    full = lambda shape: pl.BlockSpec(shape, lambda b, s: (0,) * len(shape))
    cache = pl.BlockSpec((None, rider.per_step, feat, npos), lambda b, s: (rider.layer, first_block + b * ns + s, 0, 0))
    return ([full(op.shape) for op in ride_ops[:5]] + [cache, cache], full((feat, db)),
            jax.ShapeDtypeStruct((feat, db), F32))


def _sample_out_kernel(at_a_ref, at_b_ref, co_ref, g_ref, x_ref, w_ref, lg_ref, lb_ref, y_ref, *, alpha):
    gated = ((at_a_ref[...] + at_b_ref[...]).T * g_ref[...]).astype(BF16)
    out = (jnp.dot(co_ref[...].astype(BF16), w_ref[0:CONV_W, :], preferred_element_type=F32)
           + jnp.dot(gated, w_ref[CONV_W:D_MODEL, :], preferred_element_type=F32))
    y_ref[...] = _layer_norm(alpha * x_ref[...] + out, lg_ref[...], lb_ref[...])


def _sample_out_call(at_a, at_b, co, gate, xs, w_out_l, g_l, b_l, alpha):
    return pl.pallas_call(
        functools.partial(_sample_out_kernel, alpha=alpha),
        out_shape=jax.ShapeDtypeStruct(xs.shape, F32),
        compiler_params=pltpu.CompilerParams(vmem_limit_bytes=VMEM_LIMIT),
        name="sample_out",
    )(at_a, at_b, co, gate, xs, w_out_l, g_l, b_l)


def _sample_bias(rel_bias, n_past):
    def table(dist):
        mult = sum(((dist % d == 0) & (dist <= w)).astype(np.float32) for w, d in DILATIONS)
        b = _bias_lookup(rel_bias, dist) + np.log(np.maximum(mult, 1.0))[:, None]
        return jnp.where((mult > 0)[:, None], b, NEG).T
    cache = table(n_past - np.arange(n_past))
    own = jnp.broadcast_to(table(np.zeros((1,), np.int32)), (N_HEADS, LANES))
    return cache, own


def kernel(x_prompt, x_sample, cache_k, cache_v, state_conv, w_in, conv_w, w_out, ln_g, ln_b, rel_bias):
    depth = w_in.shape[0]
    bsz, slen, _ = x_prompt.shape
    db, n_past = cache_k.shape[1], cache_k.shape[2]
    assert x_sample.shape[1] == 1
    alpha = (2 * depth) ** 0.25
    keep = min(MAX_WINDOW, slen)
    w_in_b = w_in.astype(BF16)
    w_out_b = w_out.astype(BF16)
    ln_g2 = ln_g.reshape(depth, 1, D_MODEL)
    ln_b2 = ln_b.reshape(depth, 1, D_MODEL)
    bias = _prompt_bias(rel_bias)

    from_fp = lambda t, lead: jnp.transpose(t.reshape(lead + (N_HEADS, HEAD_DIM, t.shape[-1])),
                                            tuple(range(len(lead))) + (len(lead) + 2, len(lead), len(lead) + 1))

    ck = jnp.transpose(cache_k, (0, 1, 3, 4, 2)).reshape(depth, db, ATTN_W, n_past)
    cv = jnp.transpose(cache_v, (0, 1, 3, 4, 2)).reshape(depth, db, ATTN_W, n_past)
    sbias, sbias0 = _sample_bias(rel_bias, n_past)
    steps = bsz * (slen // SEQ_TILE)
    per_step = db // (2 * steps)
    assert 2 * steps * per_step == db

    hp = x_prompt
    hs = x_sample.reshape(db, D_MODEL)
    kp_l, vp_l, cp_l, ks_l, vs_l, cs_l = [], [], [], [], [], []
    for l in range(depth):
        s0, s1 = state_conv[l, :, 0, :], state_conv[l, :, 1, :]
        qt, kts, vts, cos, gates, u = _sample_proj_call(hs, w_in_b[l], conv_w[l], s0, s1)
        ride_ops = (qt, kts, vts, sbias, sbias0, ck, cv)
        q, k, v, gate, co, kt, vt, cs, at_a = _proj_call(hp, w_in_b[l], conv_w[l], keep, ride_ops,
                                                         _Rider(l, 0, per_step))
        att = _attn_call(q, k, v, bias)
        hp, at_b = _out_call(att, gate, co, hp, w_out_b[l], ln_g2[l], ln_b2[l], alpha, ride_ops,
                             _Rider(l, steps * per_step, per_step))
        hs = _sample_out_call(at_a, at_b, cos, gates, hs, w_out_b[l], ln_g2[l], ln_b2[l], alpha)
        kp_l.append(from_fp(kt, (bsz,)))
        vp_l.append(from_fp(vt, (bsz,)))
        cp_l.append(cs)
        ks_l.append(from_fp(kts, ()).reshape(db, 1, N_HEADS, HEAD_DIM))
        vs_l.append(from_fp(vts, ()).reshape(db, 1, N_HEADS, HEAD_DIM))
        cs_l.append(jnp.stack([s1, u], axis=1))

    return (hp, hs.reshape(db, 1, D_MODEL), jnp.stack(kp_l), jnp.stack(vp_l), jnp.stack(cp_l),
            jnp.stack(ks_l), jnp.stack(vs_l), jnp.stack(cs_l))
```

```python
import functools
import math
from typing import NamedTuple

import jax
import jax.numpy as jnp
import numpy as np
from jax import lax
from jax.experimental import pallas as pl
from jax.experimental.pallas import tpu as pltpu

F32 = jnp.float32
BF16 = jnp.bfloat16

D_MODEL = 1024
HEAD_DIM = 64
N_HEADS = 8
ATTN_W = N_HEADS * HEAD_DIM
CONV_W = D_MODEL - ATTN_W
CONV_K = 3
DILATIONS = ((128, 1), (512, 4), (2048, 16))
MAX_WINDOW = 2048
N_BUCKETS = 32
MAX_DISTANCE = MAX_WINDOW
BLOCK = 128
LN_EPS = 1e-5
SCALE = HEAD_DIM ** -0.5
LOG2E = math.log2(math.e)
NEG = -1e30

LANES = 128
SEQ_TILE = 256
VMEM_LIMIT = 56 * 1024 * 1024


def _silu(x):
    return x * (1.0 / (1.0 + jnp.exp(-x)))


def _layer_norm(y, g, b):
    mu = jnp.mean(y, axis=-1, keepdims=True)
    yc = y - mu
    var = jnp.mean(yc * yc, axis=-1, keepdims=True)
    return yc * lax.rsqrt(var + LN_EPS) * g + b


def _rel_bucket(dist):
    n_exact = N_BUCKETS // 2
    d = np.maximum(dist, 0)
    large = n_exact + (np.log(np.maximum(d, 1).astype(np.float32) / np.float32(n_exact))
                       / np.float32(math.log(MAX_DISTANCE / n_exact)) * np.float32(N_BUCKETS - n_exact)
                       ).astype(np.int32)
    large = np.minimum(large, N_BUCKETS - 1)
    return np.where(d < n_exact, d, large)


def _params(*sem):
    return pltpu.CompilerParams(dimension_semantics=sem, vmem_limit_bytes=VMEM_LIMIT)


def _proj_kernel(x_ref, w_ref, cw_ref, *rest, rider):
    ride_refs, (q_ref, k_ref, v_ref, g_ref, co_ref, kt_ref, vt_ref, cs_ref, at_ref, ubuf) = rest[:7], rest[7:]
    ts = x_ref.shape[0]
    s = pl.program_id(1)

    @pl.when((pl.program_id(0) == 0) & (s == 0))
    def _():
        at_ref[...] = jnp.zeros(at_ref.shape, F32)
        ubuf[0:8, :] = jnp.zeros((8, CONV_W), F32)

    _ride(ride_refs, at_ref, rider)
    xb = x_ref[...].astype(BF16)

    def col(j):
        return jnp.dot(xb, w_ref[:, j * 512:(j + 1) * 512], preferred_element_type=F32)

    u = col(2) * col(0)
    ubuf[0:8, :] = jnp.where(s > 0, ubuf[0:8, :], 0.0)
    ubuf[8:8 + ts, :] = u
    cw = cw_ref[...]
    cy = cw[0:1, :] * ubuf[6:6 + ts, :] + cw[1:2, :] * ubuf[7:7 + ts, :] + cw[2:3, :] * u
    co_ref[...] = (col(1) * cy * _silu(col(3))).astype(BF16)
    cs_ref[...] = u[ts - 2:ts, :]
    ubuf[0:8, :] = u[ts - 8:ts, :]

    q_ref[...] = col(4) * (SCALE * LOG2E)
    k = col(5)
    k_ref[...] = k
    v = col(6)
    v_ref[...] = v
    g_ref[...] = _silu(col(7)).astype(BF16)
    kt_ref[...] = k.T
    vt_ref[...] = v.T


def _proj_call(x, w_in_l, conv_w_l, keep, ride_ops, rider):
    bsz, slen, _ = x.shape
    ts = SEQ_TILE
    ns = slen // ts
    off = (slen - keep) // ts
    ride_in, ride_out, ride_shape = _ride_specs(ride_ops, rider, ns)
    tile = lambda width: pl.BlockSpec((None, ts, width), lambda b, s: (b, s, 0))
    kept = pl.BlockSpec((None, ATTN_W, ts), lambda b, s: (b, 0, jnp.maximum(s - off, 0)))
    bf = jax.ShapeDtypeStruct((bsz, slen, ATTN_W), BF16)
    fl = jax.ShapeDtypeStruct((bsz, slen, ATTN_W), F32)
    return pl.pallas_call(
        functools.partial(_proj_kernel, rider=rider),
        grid=(bsz, ns),
        in_specs=[tile(D_MODEL),
                  pl.BlockSpec((D_MODEL, 4 * CONV_W + 4 * ATTN_W), lambda b, s: (0, 0)),
                  pl.BlockSpec((CONV_K, CONV_W), lambda b, s: (0, 0))] + ride_in,
        out_specs=[tile(ATTN_W), tile(ATTN_W), tile(ATTN_W), tile(ATTN_W), tile(CONV_W), kept, kept,
                   pl.BlockSpec((None, CONV_K - 1, CONV_W), lambda b, s: (b, 0, 0)), ride_out],
        out_shape=[fl, fl, fl, bf, jax.ShapeDtypeStruct((bsz, slen, CONV_W), BF16),
                   jax.ShapeDtypeStruct((bsz, ATTN_W, keep), F32),
                   jax.ShapeDtypeStruct((bsz, ATTN_W, keep), F32),
                   jax.ShapeDtypeStruct((bsz, CONV_K - 1, CONV_W), F32), ride_shape],
        scratch_shapes=[pltpu.VMEM((ts + 8, CONV_W), F32)],
        compiler_params=_params("arbitrary", "arbitrary"),
        name="prompt_proj",
    )(x, w_in_l, conv_w_l, *ride_ops)


NT_DIMS = (((1,), (1,)), ((), ()))
BLOCKS_IN_FLIGHT = 32


def _pair_attention(q, k, v, bias, low):
    qb = q.astype(BF16)
    zero = jnp.zeros_like(qb)
    qm = jnp.concatenate([jnp.where(low, qb, zero), jnp.where(low, zero, qb)], axis=0)
    vb = jnp.concatenate([v.astype(BF16), jnp.ones((2 * BLOCK, LANES), BF16)], axis=1)
    sc = lax.dot_general(qm, k.astype(BF16), NT_DIMS, preferred_element_type=F32) + bias
    m = jnp.max(sc, axis=1, keepdims=True)
    out = jnp.dot(jnp.exp2(sc - m).astype(BF16), vb, preferred_element_type=F32)
    per_head = lambda t: jnp.where(low, t[0:BLOCK], t[BLOCK:2 * BLOCK])
    return (per_head(out[:, 0:LANES]), per_head(jnp.broadcast_to(m, (2 * BLOCK, LANES))),
            per_head(out[:, LANES:2 * LANES]))


def _merge(a, b):
    m = jnp.maximum(a[1], b[1])
    e_a = jnp.exp2(a[1] - m)
    e_b = jnp.exp2(b[1] - m)
    return a[0] * e_a + b[0] * e_b, m, a[2] * e_a + b[2] * e_b


def _attn_kernel(q_ref, k_ref, v_ref, bias_ref, o_ref, r4, acc1, acc4):
    slen = q_ref.shape[0]
    n4 = slen // 4
    nb4 = n4 // BLOCK
    nb16 = slen // 16 // BLOCK
    low = lax.broadcasted_iota(jnp.int32, (BLOCK, LANES), 1) < HEAD_DIM

    def load(acc, rows):
        return acc[0, rows, :], acc[1, rows, :], acc[2, rows, :]

    def store(acc, rows, part):
        for a in range(3):
            acc[a, rows, :] = part[a]

    def regroup(i):
        c, jj = i // nb4, i % nb4
        src = pl.ds(c + 4 * BLOCK * jj, BLOCK, stride=4)
        dst = pl.ds(pl.multiple_of(c * n4 + jj * BLOCK, BLOCK), BLOCK)
        r4[0, dst, :] = q_ref[src, :]
        r4[1, dst, :] = k_ref[src, :]
        r4[2, dst, :] = v_ref[src, :]

    def branch1(j, carry):
        regroup(j)
        t = jnp.minimum(j, 1)
        qs = pl.ds(pl.multiple_of(j * BLOCK, BLOCK), BLOCK)
        ks = pl.ds(pl.multiple_of(jnp.maximum(j - 1, 0) * BLOCK, BLOCK), 2 * BLOCK)
        store(acc1, qs, _pair_attention(q_ref[qs, :], k_ref[ks, :], v_ref[ks, :], bias_ref[0, t], low))
        return carry

    assert slen // BLOCK == 4 * nb4
    lax.fori_loop(0, slen // BLOCK, branch1, 0, unroll=BLOCKS_IN_FLIGHT)

    def branch4(i, carry):
        c, j = i // nb4, i % nb4
        t = jnp.minimum(j, 1)
        qs = pl.ds(pl.multiple_of(c * n4 + j * BLOCK, BLOCK), BLOCK)
        ks = pl.ds(pl.multiple_of(c * n4 + jnp.maximum(j - 1, 0) * BLOCK, BLOCK), 2 * BLOCK)
        store(acc4, qs, _pair_attention(r4[0, qs, :], r4[1, ks, :], r4[2, ks, :], bias_ref[1, t], low))
        return carry

    lax.fori_loop(0, 4 * nb4, branch4, 0, unroll=BLOCKS_IN_FLIGHT)

    def branch16(i, carry):
        res, j = i // nb16, i % nb16
        base = (res % 4) * n4 + res // 4
        t = jnp.minimum(j, 1)
        qs = pl.ds(base + 4 * BLOCK * j, BLOCK, stride=4)
        ks = pl.ds(base + 4 * BLOCK * jnp.maximum(j - 1, 0), 2 * BLOCK, stride=4)
        part = _pair_attention(r4[0, qs, :], r4[1, ks, :], r4[2, ks, :], bias_ref[2, t], low)
        store(acc4, qs, _merge(load(acc4, qs), part))
        return carry

    lax.fori_loop(0, 16 * nb16, branch16, 0, unroll=BLOCKS_IN_FLIGHT)

    def mix(i, carry):
        c, jj = i // nb4, i % nb4
        rows4 = pl.ds(pl.multiple_of(c * n4 + jj * BLOCK, BLOCK), BLOCK)
        rows1 = pl.ds(c + 4 * BLOCK * jj, BLOCK, stride=4)
        num, _, den = _merge(load(acc4, rows4), load(acc1, rows1))
        o_ref[rows1, :] = num / den
        return carry

    lax.fori_loop(0, 4 * nb4, mix, 0, unroll=4)


def _attn_call(q, k, v, bias):
    bsz, slen, _ = q.shape
    assert slen % (16 * 2 * BLOCK) == 0
    n_pairs = ATTN_W // LANES
    blk = pl.BlockSpec((None, slen, LANES), lambda b, p: (b, 0, p))
    return pl.pallas_call(
        _attn_kernel,
        grid=(bsz, n_pairs),
        in_specs=[blk, blk, blk,
                  pl.BlockSpec((len(DILATIONS), 2, 2 * BLOCK, 2 * BLOCK), lambda b, p: (0, 0, p, 0))],
        out_specs=blk,
        out_shape=jax.ShapeDtypeStruct((bsz, slen, ATTN_W), F32),
        scratch_shapes=[pltpu.VMEM((3, slen, LANES), F32), pltpu.VMEM((3, slen, LANES), F32),
                        pltpu.VMEM((3, slen, LANES), F32)],
        compiler_params=_params("arbitrary", "arbitrary"),
        name="prompt_attn",
    )(q, k, v, bias)


def _bias_lookup(rel_bias, dist):
    onehot = (_rel_bucket(dist)[..., None] == np.arange(N_BUCKETS)).astype(np.float32)
    return jnp.einsum('...i,ih->...h', onehot, rel_bias.astype(F32), precision=lax.Precision.HIGHEST)


def _prompt_bias(rel_bias):
    steps = BLOCK
    qi = np.arange(BLOCK)[:, None]
    kj = np.arange(2 * BLOCK)[None, :]
    step1 = qi + BLOCK - kj
    ok1 = (step1 >= 0) & (step1 <= steps)
    step0 = qi - kj
    ok0 = (step0 >= 0) & (kj < BLOCK)

    def table(step, ok, dil):
        b = _bias_lookup(rel_bias, np.clip(step, 0, steps) * dil)
        b = jnp.transpose(jnp.where(ok[:, :, None], b * LOG2E, NEG), (2, 0, 1))
        return b.reshape(N_HEADS * BLOCK, 2 * BLOCK)

    return jnp.stack([jnp.stack([table(step0, ok0, d), table(step1, ok1, d)]) for _, d in DILATIONS])


def _out_kernel(att_ref, g_ref, co_ref, x_ref, w_ref, lg_ref, lb_ref, *rest, alpha, rider):
    ride_refs, (y_ref, at_ref) = rest[:7], rest[7:]

    @pl.when((pl.program_id(0) == 0) & (pl.program_id(1) == 0))
    def _():
        at_ref[...] = jnp.zeros(at_ref.shape, F32)

    _ride(ride_refs, at_ref, rider)
    gated = (att_ref[...] * g_ref[...].astype(F32)).astype(BF16)
    out = (jnp.dot(co_ref[...], w_ref[0:CONV_W, :], preferred_element_type=F32)
           + jnp.dot(gated, w_ref[CONV_W:D_MODEL, :], preferred_element_type=F32))
    y_ref[...] = _layer_norm(alpha * x_ref[...] + out, lg_ref[...], lb_ref[...])


def _out_call(att, gate, co, x, w_out_l, g_l, b_l, alpha, ride_ops, rider):
    bsz, slen, _ = x.shape
    ts = SEQ_TILE
    ride_in, ride_out, ride_shape = _ride_specs(ride_ops, rider, slen // ts)
    tile = lambda width: pl.BlockSpec((None, ts, width), lambda b, s: (b, s, 0))
    full = lambda shape: pl.BlockSpec(shape, lambda b, s: (0,) * len(shape))
    return pl.pallas_call(
        functools.partial(_out_kernel, alpha=alpha, rider=rider),
        grid=(bsz, slen // ts),
        in_specs=[tile(ATTN_W), tile(ATTN_W), tile(CONV_W), tile(D_MODEL),
                  full((D_MODEL, D_MODEL)), full((1, D_MODEL)), full((1, D_MODEL))] + ride_in,
        out_specs=[tile(D_MODEL), ride_out],
        out_shape=[jax.ShapeDtypeStruct((bsz, slen, D_MODEL), F32), ride_shape],
        compiler_params=_params("arbitrary", "arbitrary"),
        name="prompt_out",
    )(att, gate, co, x, w_out_l, g_l, b_l, *ride_ops)


def _sample_proj_kernel(x_ref, w_ref, cw_ref, s0_ref, s1_ref, qt_ref, kt_ref, vt_ref, co_ref, g_ref, cs_ref):
    xb = x_ref[...].astype(BF16)

    def col(c):
        return jnp.dot(xb, w_ref[:, c * 512:(c + 1) * 512], preferred_element_type=F32)

    u = col(2) * col(0)
    cw = cw_ref[...]
    cy = cw[0:1, :] * s0_ref[...] + cw[1:2, :] * s1_ref[...] + cw[2:3, :] * u
    co_ref[...] = col(1) * cy * _silu(col(3))
    cs_ref[...] = u
    qt_ref[...] = (col(4) * SCALE).T
    kt_ref[...] = col(5).T
    vt_ref[...] = col(6).T
    g_ref[...] = _silu(col(7))


def _sample_proj_call(xs, w_in_l, conv_w_l, s0, s1):
    db = xs.shape[0]
    tr = jax.ShapeDtypeStruct((ATTN_W, db), F32)
    row = jax.ShapeDtypeStruct((db, ATTN_W), F32)
    return pl.pallas_call(
        _sample_proj_kernel,
        out_shape=[tr, tr, tr, row, row, row],
        compiler_params=pltpu.CompilerParams(vmem_limit_bytes=VMEM_LIMIT),
        name="sample_proj",
    )(xs, w_in_l, conv_w_l, s0, s1)


def _sample_attend(elem, qt_ref, kt_ref, vt_ref, sb_ref, sb0_ref, kc_ref, vc_ref, at_ref):
    db = qt_ref.shape[1]
    sel = lax.broadcasted_iota(jnp.int32, (HEAD_DIM, db), 1) == elem
    for h in range(N_HEADS):
        rows = slice(h * HEAD_DIM, (h + 1) * HEAD_DIM)
        pick = lambda ref: jnp.sum(jnp.where(sel, ref[rows, :], 0.0), axis=1, keepdims=True)
        qc, kn, vn = pick(qt_ref), pick(kt_ref), pick(vt_ref)
        sc = jnp.sum(kc_ref[rows, :] * qc, axis=0, keepdims=True) + sb_ref[h:h + 1, :]
        s_self = jnp.sum(qc * kn, axis=0, keepdims=True) + sb0_ref[h:h + 1, 0:1]
        m = jnp.maximum(jnp.max(sc, axis=1, keepdims=True), s_self)
        pe = jnp.exp(sc - m)
        p_self = jnp.exp(s_self - m)
        den = jnp.sum(pe, axis=1, keepdims=True) + p_self
        num = jnp.sum(vc_ref[rows, :] * pe, axis=1, keepdims=True) + p_self * vn
        at_ref[rows, :] = jnp.where(sel, num / den, at_ref[rows, :])


class _Rider(NamedTuple):
    layer: int
    first: int
    per_step: int


def _ride(ride_refs, at_ref, rider):
    qt_ref, kt_ref, vt_ref, sb_ref, sb0_ref, kc_ref, vc_ref = ride_refs
    step = pl.program_id(0) * pl.num_programs(1) + pl.program_id(1)
    for bi in range(rider.per_step):
        _sample_attend(rider.first + step * rider.per_step + bi, qt_ref, kt_ref, vt_ref, sb_ref, sb0_ref,
                       kc_ref.at[bi], vc_ref.at[bi], at_ref)


def _ride_specs(ride_ops, rider, ns):
    ck = ride_ops[5]
    db, feat, npos = ck.shape[1], ck.shape[2], ck.shape[3]
    assert rider.first % rider.per_step == 0
    first_block = rider.first // rider.per_step
    full = lambda shape: pl.BlockSpec(shape, lambda b, s: (0,) * len(shape))
    cache = pl.BlockSpec((None, rider.per_step, feat, npos), lambda b, s: (rider.layer, first_block + b * ns + s, 0, 0))
    return ([full(op.shape) for op in ride_ops[:5]] + [cache, cache], full((feat, db)),
            jax.ShapeDtypeStruct((feat, db), F32))


def _sample_out_kernel(at_a_ref, at_b_ref, co_ref, g_ref, x_ref, w_ref, lg_ref, lb_ref, y_ref, *, alpha):
    gated = ((at_a_ref[...] + at_b_ref[...]).T * g_ref[...]).astype(BF16)
    out = (jnp.dot(co_ref[...].astype(BF16), w_ref[0:CONV_W, :], preferred_element_type=F32)
           + jnp.dot(gated, w_ref[CONV_W:D_MODEL, :], preferred_element_type=F32))
    y_ref[...] = _layer_norm(alpha * x_ref[...] + out, lg_ref[...], lb_ref[...])


def _sample_out_call(at_a, at_b, co, gate, xs, w_out_l, g_l, b_l, alpha):
    return pl.pallas_call(
        functools.partial(_sample_out_kernel, alpha=alpha),
        out_shape=jax.ShapeDtypeStruct(xs.shape, F32),
        compiler_params=pltpu.CompilerParams(vmem_limit_bytes=VMEM_LIMIT),
        name="sample_out",
    )(at_a, at_b, co, gate, xs, w_out_l, g_l, b_l)


def _sample_bias(rel_bias, n_past):
    def table(dist):
        mult = sum(((dist % d == 0) & (dist <= w)).astype(np.float32) for w, d in DILATIONS)
        b = _bias_lookup(rel_bias, dist) + np.log(np.maximum(mult, 1.0))[:, None]
        return jnp.where((mult > 0)[:, None], b, NEG).T
    cache = table(n_past - np.arange(n_past))
    own = jnp.broadcast_to(table(np.zeros((1,), np.int32)), (N_HEADS, LANES))
    return cache, own


def kernel(x_prompt, x_sample, cache_k, cache_v, state_conv, w_in, conv_w, w_out, ln_g, ln_b, rel_bias):
    depth = w_in.shape[0]
    bsz, slen, _ = x_prompt.shape
    db, n_past = cache_k.shape[1], cache_k.shape[2]
    assert x_sample.shape[1] == 1
    alpha = (2 * depth) ** 0.25
    keep = min(MAX_WINDOW, slen)
    w_in_b = w_in.astype(BF16)
    w_out_b = w_out.astype(BF16)
    ln_g2 = ln_g.reshape(depth, 1, D_MODEL)
    ln_b2 = ln_b.reshape(depth, 1, D_MODEL)
    bias = _prompt_bias(rel_bias)

    from_fp = lambda t, lead: jnp.transpose(t.reshape(lead + (N_HEADS, HEAD_DIM, t.shape[-1])),
                                            tuple(range(len(lead))) + (len(lead) + 2, len(lead), len(lead) + 1))

    ck = jnp.transpose(cache_k, (0, 1, 3, 4, 2)).reshape(depth, db, ATTN_W, n_past)
    cv = jnp.transpose(cache_v, (0, 1, 3, 4, 2)).reshape(depth, db, ATTN_W, n_past)
    sbias, sbias0 = _sample_bias(rel_bias, n_past)
    steps = bsz * (slen // SEQ_TILE)
    per_step = db // (2 * steps)
    assert 2 * steps * per_step == db

    hp = x_prompt
    hs = x_sample.reshape(db, D_MODEL)
    kp_l, vp_l, cp_l, ks_l, vs_l, cs_l = [], [], [], [], [], []
    for l in range(depth):
        s0, s1 = state_conv[l, :, 0, :], state_conv[l, :, 1, :]
        qt, kts, vts, cos, gates, u = _sample_proj_call(hs, w_in_b[l], conv_w[l], s0, s1)
        ride_ops = (qt, kts, vts, sbias, sbias0, ck, cv)
        q, k, v, gate, co, kt, vt, cs, at_a = _proj_call(hp, w_in_b[l], conv_w[l], keep, ride_ops,
                                                         _Rider(l, 0, per_step))
        att = _attn_call(q, k, v, bias)
        hp, at_b = _out_call(att, gate, co, hp, w_out_b[l], ln_g2[l], ln_b2[l], alpha, ride_ops,
                             _Rider(l, steps * per_step, per_step))
        hs = _sample_out_call(at_a, at_b, cos, gates, hs, w_out_b[l], ln_g2[l], ln_b2[l], alpha)
        kp_l.append(from_fp(kt, (bsz,)))
        vp_l.append(from_fp(vt, (bsz,)))
        cp_l.append(cs)
        ks_l.append(from_fp(kts, ()).reshape(db, 1, N_HEADS, HEAD_DIM))
        vs_l.append(from_fp(vts, ()).reshape(db, 1, N_HEADS, HEAD_DIM))
        cs_l.append(jnp.stack([s1, u], axis=1))

    return (hp, hs.reshape(db, 1, D_MODEL), jnp.stack(kp_l), jnp.stack(vp_l), jnp.stack(cp_l),
            jnp.stack(ks_l), jnp.stack(vs_l), jnp.stack(cs_l))
```

```python
import functools
import math
from typing import NamedTuple

import jax
import jax.numpy as jnp
import numpy as np
from jax import lax
from jax.experimental import pallas as pl
from jax.experimental.pallas import tpu as pltpu

F32 = jnp.float32
BF16 = jnp.bfloat16

D_MODEL = 1024
HEAD_DIM = 64
N_HEADS = 8
ATTN_W = N_HEADS * HEAD_DIM
CONV_W = D_MODEL - ATTN_W
CONV_K = 3
DILATIONS = ((128, 1), (512, 4), (2048, 16))
MAX_WINDOW = 2048
N_BUCKETS = 32
MAX_DISTANCE = MAX_WINDOW
BLOCK = 128
LN_EPS = 1e-5
SCALE = HEAD_DIM ** -0.5
LOG2E = math.log2(math.e)
NEG = -1e30

LANES = 128
SEQ_TILE = 256
VMEM_LIMIT = 56 * 1024 * 1024


def _silu(x):
    return x * (1.0 / (1.0 + jnp.exp(-x)))


def _layer_norm(y, g, b):
    mu = jnp.mean(y, axis=-1, keepdims=True)
    yc = y - mu
    var = jnp.mean(yc * yc, axis=-1, keepdims=True)
    return yc * lax.rsqrt(var + LN_EPS) * g + b


def _rel_bucket(dist):
    n_exact = N_BUCKETS // 2
    d = np.maximum(dist, 0)
    large = n_exact + (np.log(np.maximum(d, 1).astype(np.float32) / np.float32(n_exact))
                       / np.float32(math.log(MAX_DISTANCE / n_exact)) * np.float32(N_BUCKETS - n_exact)
                       ).astype(np.int32)
    large = np.minimum(large, N_BUCKETS - 1)
    return np.where(d < n_exact, d, large)


def _params(*sem):
    return pltpu.CompilerParams(dimension_semantics=sem, vmem_limit_bytes=VMEM_LIMIT)


def _proj_kernel(x_ref, w_ref, cw_ref, *rest, rider):
    ride_refs, (qkv_ref, gc_ref, kt_ref, vt_ref, cs_ref, at_ref, ubuf) = rest[:7], rest[7:]
    ts = x_ref.shape[0]
    s = pl.program_id(1)

    @pl.when((pl.program_id(0) == 0) & (s == 0))
    def _():
        at_ref[...] = jnp.zeros(at_ref.shape, F32)
        ubuf[0:8, :] = jnp.zeros((8, CONV_W), F32)

    _ride(ride_refs, at_ref, rider)
    xb = x_ref[...].astype(BF16)

    def col(j):
        return jnp.dot(xb, w_ref[:, j * 512:(j + 1) * 512], preferred_element_type=F32)

    u = col(2) * col(0)
    ubuf[0:8, :] = jnp.where(s > 0, ubuf[0:8, :], 0.0)
    ubuf[8:8 + ts, :] = u
    cw = cw_ref[...]
    cy = cw[0:1, :] * ubuf[6:6 + ts, :] + cw[1:2, :] * ubuf[7:7 + ts, :] + cw[2:3, :] * u
    gc_ref[:, ATTN_W:ATTN_W + CONV_W] = (col(1) * cy * _silu(col(3))).astype(BF16)
    cs_ref[...] = u[ts - 2:ts, :]
    ubuf[0:8, :] = u[ts - 8:ts, :]

    qkv_ref[:, 0:ATTN_W] = col(4) * (SCALE * LOG2E)
    k = col(5)
    qkv_ref[:, ATTN_W:2 * ATTN_W] = k
    v = col(6)
    qkv_ref[:, 2 * ATTN_W:3 * ATTN_W] = v
    gc_ref[:, 0:ATTN_W] = _silu(col(7)).astype(BF16)
    kt_ref[...] = k.T
    vt_ref[...] = v.T


def _proj_call(x, w_in_b, conv_w, layer, keep, ride_ops, rider):
    bsz, slen, _ = x.shape
    ts = SEQ_TILE
    ns = slen // ts
    off = (slen - keep) // ts
    ride_in, ride_out, ride_shape = _ride_specs(ride_ops, rider, ns)
    tile = lambda width: pl.BlockSpec((None, ts, width), lambda b, s: (b, s, 0))
    kept = pl.BlockSpec((None, ATTN_W, ts), lambda b, s: (b, 0, jnp.maximum(s - off, 0)))
    return pl.pallas_call(
        functools.partial(_proj_kernel, rider=rider),
        grid=(bsz, ns),
        in_specs=[tile(D_MODEL),
                  pl.BlockSpec((None, D_MODEL, 4 * CONV_W + 4 * ATTN_W), lambda b, s: (layer, 0, 0)),
                  pl.BlockSpec((None, CONV_K, CONV_W), lambda b, s: (layer, 0, 0))] + ride_in,
        out_specs=[tile(3 * ATTN_W), tile(ATTN_W + CONV_W), kept, kept,
                   pl.BlockSpec((None, CONV_K - 1, CONV_W), lambda b, s: (b, 0, 0)), ride_out],
        out_shape=[jax.ShapeDtypeStruct((bsz, slen, 3 * ATTN_W), F32),
                   jax.ShapeDtypeStruct((bsz, slen, ATTN_W + CONV_W), BF16),
                   jax.ShapeDtypeStruct((bsz, ATTN_W, keep), F32),
                   jax.ShapeDtypeStruct((bsz, ATTN_W, keep), F32),
                   jax.ShapeDtypeStruct((bsz, CONV_K - 1, CONV_W), F32), ride_shape],
        scratch_shapes=[pltpu.VMEM((ts + 8, CONV_W), F32)],
        compiler_params=_params("arbitrary", "arbitrary"),
        name="prompt_proj",
    )(x, w_in_b, conv_w, *ride_ops)


NT_DIMS = (((1,), (1,)), ((), ()))
BLOCKS_IN_FLIGHT = 32


def _pair_attention(q, k, v, bias, low):
    qb = q.astype(BF16)
    zero = jnp.zeros_like(qb)
    qm = jnp.concatenate([jnp.where(low, qb, zero), jnp.where(low, zero, qb)], axis=0)
    vb = jnp.concatenate([v.astype(BF16), jnp.ones((2 * BLOCK, LANES), BF16)], axis=1)
    sc = lax.dot_general(qm, k.astype(BF16), NT_DIMS, preferred_element_type=F32) + bias
    m = jnp.max(sc, axis=1, keepdims=True)
    out = jnp.dot(jnp.exp2(sc - m).astype(BF16), vb, preferred_element_type=F32)
    per_head = lambda t: jnp.where(low, t[0:BLOCK], t[BLOCK:2 * BLOCK])
    return (per_head(out[:, 0:LANES]), per_head(jnp.broadcast_to(m, (2 * BLOCK, LANES))),
            per_head(out[:, LANES:2 * LANES]))


def _merge(a, b):
    m = jnp.maximum(a[1], b[1])
    e_a = jnp.exp2(a[1] - m)
    e_b = jnp.exp2(b[1] - m)
    return a[0] * e_a + b[0] * e_b, m, a[2] * e_a + b[2] * e_b


def _attn_kernel(q_ref, k_ref, v_ref, bias_ref, o_ref, r4, acc1, acc4):
    slen = q_ref.shape[0]
    n4 = slen // 4
    nb4 = n4 // BLOCK
    nb16 = slen // 16 // BLOCK
    low = lax.broadcasted_iota(jnp.int32, (BLOCK, LANES), 1) < HEAD_DIM

    def load(acc, rows):
        return acc[0, rows, :], acc[1, rows, :], acc[2, rows, :]

    def store(acc, rows, part):
        for a in range(3):
            acc[a, rows, :] = part[a]

    def regroup(i):
        c, jj = i // nb4, i % nb4
        src = pl.ds(c + 4 * BLOCK * jj, BLOCK, stride=4)
        dst = pl.ds(pl.multiple_of(c * n4 + jj * BLOCK, BLOCK), BLOCK)
        r4[0, dst, :] = q_ref[src, :]
        r4[1, dst, :] = k_ref[src, :]
        r4[2, dst, :] = v_ref[src, :]

    def branch1(j, carry):
        regroup(j)
        t = jnp.minimum(j, 1)
        qs = pl.ds(pl.multiple_of(j * BLOCK, BLOCK), BLOCK)
        ks = pl.ds(pl.multiple_of(jnp.maximum(j - 1, 0) * BLOCK, BLOCK), 2 * BLOCK)
        store(acc1, qs, _pair_attention(q_ref[qs, :], k_ref[ks, :], v_ref[ks, :], bias_ref[0, t], low))
        return carry

    assert slen // BLOCK == 4 * nb4
    lax.fori_loop(0, slen // BLOCK, branch1, 0, unroll=BLOCKS_IN_FLIGHT)

    def branch4(i, carry):
        c, j = i // nb4, i % nb4
        t = jnp.minimum(j, 1)
        qs = pl.ds(pl.multiple_of(c * n4 + j * BLOCK, BLOCK), BLOCK)
        ks = pl.ds(pl.multiple_of(c * n4 + jnp.maximum(j - 1, 0) * BLOCK, BLOCK), 2 * BLOCK)
        store(acc4, qs, _pair_attention(r4[0, qs, :], r4[1, ks, :], r4[2, ks, :], bias_ref[1, t], low))
        return carry

    lax.fori_loop(0, 4 * nb4, branch4, 0, unroll=BLOCKS_IN_FLIGHT)

    def branch16(i, carry):
        res, j = i // nb16, i % nb16
        base = (res % 4) * n4 + res // 4
        t = jnp.minimum(j, 1)
        qs = pl.ds(base + 4 * BLOCK * j, BLOCK, stride=4)
        ks = pl.ds(base + 4 * BLOCK * jnp.maximum(j - 1, 0), 2 * BLOCK, stride=4)
        part = _pair_attention(r4[0, qs, :], r4[1, ks, :], r4[2, ks, :], bias_ref[2, t], low)
        store(acc4, qs, _merge(load(acc4, qs), part))
        return carry

    lax.fori_loop(0, 16 * nb16, branch16, 0, unroll=BLOCKS_IN_FLIGHT)

    def mix(i, carry):
        c, jj = i // nb4, i % nb4
        rows4 = pl.ds(pl.multiple_of(c * n4 + jj * BLOCK, BLOCK), BLOCK)
        rows1 = pl.ds(c + 4 * BLOCK * jj, BLOCK, stride=4)
        num, _, den = _merge(load(acc4, rows4), load(acc1, rows1))
        o_ref[rows1, :] = num / den
        return carry

    lax.fori_loop(0, 4 * nb4, mix, 0, unroll=4)


def _attn_call(qkv, bias):
    bsz, slen, _ = qkv.shape
    assert slen % (16 * 2 * BLOCK) == 0
    n_pairs = ATTN_W // LANES
    blk = pl.BlockSpec((None, slen, LANES), lambda b, p: (b, 0, p))
    part = lambda a: pl.BlockSpec((None, slen, LANES), lambda b, p: (b, 0, a * n_pairs + p))
    return pl.pallas_call(
        _attn_kernel,
        grid=(bsz, n_pairs),
        in_specs=[part(0), part(1), part(2),
                  pl.BlockSpec((len(DILATIONS), 2, 2 * BLOCK, 2 * BLOCK), lambda b, p: (0, 0, p, 0))],
        out_specs=blk,
        out_shape=jax.ShapeDtypeStruct((bsz, slen, ATTN_W), F32),
        scratch_shapes=[pltpu.VMEM((3, slen, LANES), F32), pltpu.VMEM((3, slen, LANES), F32),
                        pltpu.VMEM((3, slen, LANES), F32)],
        compiler_params=_params("arbitrary", "arbitrary"),
        name="prompt_attn",
    )(qkv, qkv, qkv, bias)


def _bias_lookup(rel_bias, dist):
    onehot = (_rel_bucket(dist)[..., None] == np.arange(N_BUCKETS)).astype(np.float32)
    return jnp.einsum('...i,ih->...h', onehot, rel_bias.astype(F32), precision=lax.Precision.HIGHEST)


def _prompt_bias(rel_bias):
    steps = BLOCK
    qi = np.arange(BLOCK)[:, None]
    kj = np.arange(2 * BLOCK)[None, :]
    step1 = qi + BLOCK - kj
    ok1 = (step1 >= 0) & (step1 <= steps)
    step0 = qi - kj
    ok0 = (step0 >= 0) & (kj < BLOCK)

    def table(step, ok, dil):
        b = _bias_lookup(rel_bias, np.clip(step, 0, steps) * dil)
        b = jnp.transpose(jnp.where(ok[:, :, None], b * LOG2E, NEG), (2, 0, 1))
        return b.reshape(N_HEADS * BLOCK, 2 * BLOCK)

    return jnp.stack([jnp.stack([table(step0, ok0, d), table(step1, ok1, d)]) for _, d in DILATIONS])


def _out_kernel(att_ref, gc_ref, x_ref, w_ref, lg_ref, lb_ref, *rest, alpha, rider):
    ride_refs, (y_ref, at_ref) = rest[:7], rest[7:]

    @pl.when((pl.program_id(0) == 0) & (pl.program_id(1) == 0))
    def _():
        at_ref[...] = jnp.zeros(at_ref.shape, F32)

    _ride(ride_refs, at_ref, rider)
    gated = (att_ref[...] * gc_ref[:, 0:ATTN_W].astype(F32)).astype(BF16)
    out = (jnp.dot(gc_ref[:, ATTN_W:ATTN_W + CONV_W], w_ref[0:CONV_W, :], preferred_element_type=F32)
           + jnp.dot(gated, w_ref[CONV_W:D_MODEL, :], preferred_element_type=F32))
    y_ref[...] = _layer_norm(alpha * x_ref[...] + out, lg_ref[...], lb_ref[...])


def _out_call(att, gc, x, w_out_b, ln_g2, ln_b2, layer, alpha, ride_ops, rider):
    bsz, slen, _ = x.shape
    ts = SEQ_TILE
    ride_in, ride_out, ride_shape = _ride_specs(ride_ops, rider, slen // ts)
    tile = lambda width: pl.BlockSpec((None, ts, width), lambda b, s: (b, s, 0))
    per_layer = lambda *shape: pl.BlockSpec((None,) + shape, lambda b, s: (layer,) + (0,) * len(shape))
    return pl.pallas_call(
        functools.partial(_out_kernel, alpha=alpha, rider=rider),
        grid=(bsz, slen // ts),
        in_specs=[tile(ATTN_W), tile(ATTN_W + CONV_W), tile(D_MODEL),
                  per_layer(D_MODEL, D_MODEL), per_layer(1, D_MODEL), per_layer(1, D_MODEL)] + ride_in,
        out_specs=[tile(D_MODEL), ride_out],
        out_shape=[jax.ShapeDtypeStruct((bsz, slen, D_MODEL), F32), ride_shape],
        compiler_params=_params("arbitrary", "arbitrary"),
        name="prompt_out",
    )(att, gc, x, w_out_b, ln_g2, ln_b2, *ride_ops)


def _sample_proj_kernel(x_ref, w_ref, cw_ref, s0_ref, s1_ref, qt_ref, kt_ref, vt_ref, co_ref, g_ref, cs_ref):
    xb = x_ref[...].astype(BF16)

    def col(c):
        return jnp.dot(xb, w_ref[:, c * 512:(c + 1) * 512], preferred_element_type=F32)

    u = col(2) * col(0)
    cw = cw_ref[...]
    cy = cw[0:1, :] * s0_ref[...] + cw[1:2, :] * s1_ref[...] + cw[2:3, :] * u
    co_ref[...] = col(1) * cy * _silu(col(3))
    cs_ref[...] = u
    qt_ref[...] = (col(4) * SCALE).T
    kt_ref[...] = col(5).T
    vt_ref[...] = col(6).T
    g_ref[...] = _silu(col(7))


def _sample_proj_call(xs, w_in_b, conv_w, layer, s0, s1):
    db = xs.shape[0]
    tr = jax.ShapeDtypeStruct((ATTN_W, db), F32)
    row = jax.ShapeDtypeStruct((db, ATTN_W), F32)
    full = lambda shape: pl.BlockSpec(shape, lambda i: (0,) * len(shape))
    per_layer = lambda *shape: pl.BlockSpec((None,) + shape, lambda i: (layer,) + (0,) * len(shape))
    return pl.pallas_call(
        _sample_proj_kernel,
        grid=(1,),
        in_specs=[full(xs.shape), per_layer(*w_in_b.shape[1:]), per_layer(*conv_w.shape[1:]),
                  full(s0.shape), full(s1.shape)],
        out_specs=[full(tr.shape)] * 3 + [full(row.shape)] * 3,
        out_shape=[tr, tr, tr, row, row, row],
        compiler_params=_params("arbitrary"),
        name="sample_proj",
    )(xs, w_in_b, conv_w, s0, s1)


def _sample_attend(elem, qt_ref, kt_ref, vt_ref, sb_ref, sb0_ref, kc_ref, vc_ref, at_ref):
    db = qt_ref.shape[1]
    sel = lax.broadcasted_iota(jnp.int32, (HEAD_DIM, db), 1) == elem
    for h in range(N_HEADS):
        rows = slice(h * HEAD_DIM, (h + 1) * HEAD_DIM)
        pick = lambda ref: jnp.sum(jnp.where(sel, ref[rows, :], 0.0), axis=1, keepdims=True)
        qc, kn, vn = pick(qt_ref), pick(kt_ref), pick(vt_ref)
        sc = jnp.sum(kc_ref[rows, :] * qc, axis=0, keepdims=True) + sb_ref[h:h + 1, :]
        s_self = jnp.sum(qc * kn, axis=0, keepdims=True) + sb0_ref[h:h + 1, 0:1]
        m = jnp.maximum(jnp.max(sc, axis=1, keepdims=True), s_self)
        pe = jnp.exp(sc - m)
        p_self = jnp.exp(s_self - m)
        den = jnp.sum(pe, axis=1, keepdims=True) + p_self
        num = jnp.sum(vc_ref[rows, :] * pe, axis=1, keepdims=True) + p_self * vn
        at_ref[rows, :] = jnp.where(sel, num / den, at_ref[rows, :])


class _Rider(NamedTuple):
    layer: int
    first: int
    per_step: int


def _ride(ride_refs, at_ref, rider):
    qt_ref, kt_ref, vt_ref, sb_ref, sb0_ref, kc_ref, vc_ref = ride_refs
    step = pl.program_id(0) * pl.num_programs(1) + pl.program_id(1)
    for bi in range(rider.per_step):
        _sample_attend(rider.first + step * rider.per_step + bi, qt_ref, kt_ref, vt_ref, sb_ref, sb0_ref,
                       kc_ref.at[bi], vc_ref.at[bi], at_ref)


def _ride_specs(ride_ops, rider, ns):
    ck = ride_ops[5]
    db, feat, npos = ck.shape[1], ck.shape[2], ck.shape[3]
    assert rider.first % rider.per_step == 0
    first_block = rider.first // rider.per_step
    full = lambda shape: pl.BlockSpec(shape, lambda b, s: (0,) * len(shape))
    cache = pl.BlockSpec((None, rider.per_step, feat, npos), lambda b, s: (rider.layer, first_block + b * ns + s, 0, 0))
    return ([full(op.shape) for op in ride_ops[:5]] + [cache, cache], full((feat, db)),
            jax.ShapeDtypeStruct((feat, db), F32))


def _sample_out_kernel(at_a_ref, at_b_ref, co_ref, g_ref, x_ref, w_ref, lg_ref, lb_ref, y_ref, *, alpha):
    gated = ((at_a_ref[...] + at_b_ref[...]).T * g_ref[...]).astype(BF16)
    out = (jnp.dot(co_ref[...].astype(BF16), w_ref[0:CONV_W, :], preferred_element_type=F32)
           + jnp.dot(gated, w_ref[CONV_W:D_MODEL, :], preferred_element_type=F32))
    y_ref[...] = _layer_norm(alpha * x_ref[...] + out, lg_ref[...], lb_ref[...])


def _sample_out_call(at_a, at_b, co, gate, xs, w_out_b, ln_g2, ln_b2, layer, alpha):
    full = lambda shape: pl.BlockSpec(shape, lambda i: (0,) * len(shape))
    per_layer = lambda *shape: pl.BlockSpec((None,) + shape, lambda i: (layer,) + (0,) * len(shape))
    return pl.pallas_call(
        functools.partial(_sample_out_kernel, alpha=alpha),
        grid=(1,),
        in_specs=[full(at_a.shape), full(at_b.shape), full(co.shape), full(gate.shape), full(xs.shape),
                  per_layer(D_MODEL, D_MODEL), per_layer(1, D_MODEL), per_layer(1, D_MODEL)],
        out_specs=full(xs.shape),
        out_shape=jax.ShapeDtypeStruct(xs.shape, F32),
        compiler_params=_params("arbitrary"),
        name="sample_out",
    )(at_a, at_b, co, gate, xs, w_out_b, ln_g2, ln_b2)


def _sample_bias(rel_bias, n_past):
    def table(dist):
        mult = sum(((dist % d == 0) & (dist <= w)).astype(np.float32) for w, d in DILATIONS)
        b = _bias_lookup(rel_bias, dist) + np.log(np.maximum(mult, 1.0))[:, None]
        return jnp.where((mult > 0)[:, None], b, NEG).T
    cache = table(n_past - np.arange(n_past))
    own = jnp.broadcast_to(table(np.zeros((1,), np.int32)), (N_HEADS, LANES))
    return cache, own


def kernel(x_prompt, x_sample, cache_k, cache_v, state_conv, w_in, conv_w, w_out, ln_g, ln_b, rel_bias):
    depth = w_in.shape[0]
    bsz, slen, _ = x_prompt.shape
    db, n_past = cache_k.shape[1], cache_k.shape[2]
    assert x_sample.shape[1] == 1
    alpha = (2 * depth) ** 0.25
    keep = min(MAX_WINDOW, slen)
    w_in_b = w_in.astype(BF16)
    w_out_b = w_out.astype(BF16)
    ln_g2 = ln_g.reshape(depth, 1, D_MODEL)
    ln_b2 = ln_b.reshape(depth, 1, D_MODEL)
    bias = _prompt_bias(rel_bias)

    from_fp = lambda t, lead: jnp.transpose(t.reshape(lead + (N_HEADS, HEAD_DIM, t.shape[-1])),
                                            tuple(range(len(lead))) + (len(lead) + 2, len(lead), len(lead) + 1))

    ck = jnp.transpose(cache_k, (0, 1, 3, 4, 2)).reshape(depth, db, ATTN_W, n_past)
    cv = jnp.transpose(cache_v, (0, 1, 3, 4, 2)).reshape(depth, db, ATTN_W, n_past)
    sbias, sbias0 = _sample_bias(rel_bias, n_past)
    steps = bsz * (slen // SEQ_TILE)
    per_step = db // (2 * steps)
    assert 2 * steps * per_step == db

    hp = x_prompt
    hs = x_sample.reshape(db, D_MODEL)
    kp_l, vp_l, cp_l, ks_l, vs_l, cs_l = [], [], [], [], [], []
    for l in range(depth):
        s0, s1 = state_conv[l, :, 0, :], state_conv[l, :, 1, :]
        qt, kts, vts, cos, gates, u = _sample_proj_call(hs, w_in_b, conv_w, l, s0, s1)
        ride_ops = (qt, kts, vts, sbias, sbias0, ck, cv)
        qkv, gc, kt, vt, cs, at_a = _proj_call(hp, w_in_b, conv_w, l, keep, ride_ops, _Rider(l, 0, per_step))
        att = _attn_call(qkv, bias)
        hp, at_b = _out_call(att, gc, hp, w_out_b, ln_g2, ln_b2, l, alpha, ride_ops,
                             _Rider(l, steps * per_step, per_step))
        hs = _sample_out_call(at_a, at_b, cos, gates, hs, w_out_b, ln_g2, ln_b2, l, alpha)
        kp_l.append(from_fp(kt, (bsz,)))
        vp_l.append(from_fp(vt, (bsz,)))
        cp_l.append(cs)
        ks_l.append(from_fp(kts, ()).reshape(db, 1, N_HEADS, HEAD_DIM))
        vs_l.append(from_fp(vts, ()).reshape(db, 1, N_HEADS, HEAD_DIM))
        cs_l.append(jnp.stack([s1, u], axis=1))

    return (hp, hs.reshape(db, 1, D_MODEL), jnp.stack(kp_l), jnp.stack(vp_l), jnp.stack(cp_l),
            jnp.stack(ks_l), jnp.stack(vs_l), jnp.stack(cs_l))
```

```python
import functools
import math
from typing import NamedTuple

import jax
import jax.numpy as jnp
import numpy as np
from jax import lax
from jax.experimental import pallas as pl
from jax.experimental.pallas import tpu as pltpu

F32 = jnp.float32
BF16 = jnp.bfloat16

D_MODEL = 1024
HEAD_DIM = 64
N_HEADS = 8
ATTN_W = N_HEADS * HEAD_DIM
CONV_W = D_MODEL - ATTN_W
CONV_K = 3
DILATIONS = ((128, 1), (512, 4), (2048, 16))
MAX_WINDOW = 2048
N_BUCKETS = 32
MAX_DISTANCE = MAX_WINDOW
BLOCK = 128
LN_EPS = 1e-5
SCALE = HEAD_DIM ** -0.5
LOG2E = math.log2(math.e)
NEG = -1e30

LANES = 128
SEQ_TILE = 256
VMEM_LIMIT = 56 * 1024 * 1024


def _silu(x):
    return x * (1.0 / (1.0 + jnp.exp(-x)))


def _layer_norm(y, g, b):
    mu = jnp.mean(y, axis=-1, keepdims=True)
    yc = y - mu
    var = jnp.mean(yc * yc, axis=-1, keepdims=True)
    return yc * lax.rsqrt(var + LN_EPS) * g + b


def _rel_bucket(dist):
    n_exact = N_BUCKETS // 2
    d = np.maximum(dist, 0)
    large = n_exact + (np.log(np.maximum(d, 1).astype(np.float32) / np.float32(n_exact))
                       / np.float32(math.log(MAX_DISTANCE / n_exact)) * np.float32(N_BUCKETS - n_exact)
                       ).astype(np.int32)
    large = np.minimum(large, N_BUCKETS - 1)
    return np.where(d < n_exact, d, large)


def _params(*sem):
    return pltpu.CompilerParams(dimension_semantics=sem, vmem_limit_bytes=VMEM_LIMIT)


def _proj_kernel(x_ref, w_ref, cw_ref, *rest, rider, n_aliased):
    ride_refs, (qkv_ref, gc_ref, kt_ref, vt_ref, cs_ref, at_ref, ubuf) = rest[:7], rest[7 + n_aliased:]
    ts = x_ref.shape[0]
    s = pl.program_id(1)

    @pl.when((pl.program_id(0) == 0) & (s == 0))
    def _():
        at_ref[...] = jnp.zeros(at_ref.shape, F32)
        ubuf[0:8, :] = jnp.zeros((8, CONV_W), F32)

    _ride(ride_refs, at_ref, rider)
    xb = x_ref[...].astype(BF16)

    def col(j):
        return jnp.dot(xb, w_ref[:, j * 512:(j + 1) * 512], preferred_element_type=F32)

    u = col(2) * col(0)
    ubuf[0:8, :] = jnp.where(s > 0, ubuf[0:8, :], 0.0)
    ubuf[8:8 + ts, :] = u
    cw = cw_ref[...]
    cy = cw[0:1, :] * ubuf[6:6 + ts, :] + cw[1:2, :] * ubuf[7:7 + ts, :] + cw[2:3, :] * u
    gc_ref[:, ATTN_W:ATTN_W + CONV_W] = (col(1) * cy * _silu(col(3))).astype(BF16)
    cs_ref[...] = u[ts - 2:ts, :]
    ubuf[0:8, :] = u[ts - 8:ts, :]

    qkv_ref[:, 0:ATTN_W] = col(4) * (SCALE * LOG2E)
    k = col(5)
    qkv_ref[:, ATTN_W:2 * ATTN_W] = k
    v = col(6)
    qkv_ref[:, 2 * ATTN_W:3 * ATTN_W] = v
    gc_ref[:, 0:ATTN_W] = _silu(col(7)).astype(BF16)
    kt_ref[...] = k.T
    vt_ref[...] = v.T


def _proj_call(x, w_in_b, conv_w, layer, keep, ride_ops, rider, kv_all):
    depth = w_in_b.shape[0]
    bsz, slen, _ = x.shape
    ts = SEQ_TILE
    ns = slen // ts
    off = (slen - keep) // ts
    ride_in, ride_out, ride_shape = _ride_specs(ride_ops, rider, ns)
    tile = lambda width: pl.BlockSpec((None, ts, width), lambda b, s: (b, s, 0))
    kept = pl.BlockSpec((None, None, ATTN_W, ts), lambda b, s: (layer, b, 0, jnp.maximum(s - off, 0)))
    aliased = [] if kv_all is None else list(kv_all)
    n_in = 3 + len(ride_in)
    return pl.pallas_call(
        functools.partial(_proj_kernel, rider=rider, n_aliased=len(aliased)),
        grid=(bsz, ns),
        in_specs=[tile(D_MODEL),
                  pl.BlockSpec((None, D_MODEL, 4 * CONV_W + 4 * ATTN_W), lambda b, s: (layer, 0, 0)),
                  pl.BlockSpec((None, CONV_K, CONV_W), lambda b, s: (layer, 0, 0))] + ride_in
                 + [pl.BlockSpec(memory_space=pl.ANY)] * len(aliased),
        input_output_aliases={n_in + i: 2 + i for i in range(len(aliased))},
        out_specs=[tile(3 * ATTN_W), tile(ATTN_W + CONV_W), kept, kept,
                   pl.BlockSpec((None, CONV_K - 1, CONV_W), lambda b, s: (b, 0, 0)), ride_out],
        out_shape=[jax.ShapeDtypeStruct((bsz, slen, 3 * ATTN_W), F32),
                   jax.ShapeDtypeStruct((bsz, slen, ATTN_W + CONV_W), BF16),
                   jax.ShapeDtypeStruct((depth, bsz, ATTN_W, keep), F32),
                   jax.ShapeDtypeStruct((depth, bsz, ATTN_W, keep), F32),
                   jax.ShapeDtypeStruct((bsz, CONV_K - 1, CONV_W), F32), ride_shape],
        scratch_shapes=[pltpu.VMEM((ts + 8, CONV_W), F32)],
        compiler_params=_params("arbitrary", "arbitrary"),
        name="prompt_proj",
    )(x, w_in_b, conv_w, *ride_ops, *aliased)


NT_DIMS = (((1,), (1,)), ((), ()))
BLOCKS_IN_FLIGHT = 32


def _pair_attention(q, k, v, bias, low):
    qb = q.astype(BF16)
    zero = jnp.zeros_like(qb)
    qm = jnp.concatenate([jnp.where(low, qb, zero), jnp.where(low, zero, qb)], axis=0)
    vb = jnp.concatenate([v.astype(BF16), jnp.ones((2 * BLOCK, LANES), BF16)], axis=1)
    sc = lax.dot_general(qm, k.astype(BF16), NT_DIMS, preferred_element_type=F32) + bias
    m = jnp.max(sc, axis=1, keepdims=True)
    out = jnp.dot(jnp.exp2(sc - m).astype(BF16), vb, preferred_element_type=F32)
    per_head = lambda t: jnp.where(low, t[0:BLOCK], t[BLOCK:2 * BLOCK])
    return (per_head(out[:, 0:LANES]), per_head(jnp.broadcast_to(m, (2 * BLOCK, LANES))),
            per_head(out[:, LANES:2 * LANES]))


def _merge(a, b):
    m = jnp.maximum(a[1], b[1])
    e_a = jnp.exp2(a[1] - m)
    e_b = jnp.exp2(b[1] - m)
    return a[0] * e_a + b[0] * e_b, m, a[2] * e_a + b[2] * e_b


def _attn_kernel(q_ref, k_ref, v_ref, bias_ref, o_ref, r4, acc1, acc4):
    slen = q_ref.shape[0]
    n4 = slen // 4
    nb4 = n4 // BLOCK
    nb16 = slen // 16 // BLOCK
    low = lax.broadcasted_iota(jnp.int32, (BLOCK, LANES), 1) < HEAD_DIM

    def load(acc, rows):
        return acc[0, rows, :], acc[1, rows, :], acc[2, rows, :]

    def store(acc, rows, part):
        for a in range(3):
            acc[a, rows, :] = part[a]

    def regroup(i):
        c, jj = i // nb4, i % nb4
        src = pl.ds(c + 4 * BLOCK * jj, BLOCK, stride=4)
        dst = pl.ds(pl.multiple_of(c * n4 + jj * BLOCK, BLOCK), BLOCK)
        r4[0, dst, :] = q_ref[src, :]
        r4[1, dst, :] = k_ref[src, :]
        r4[2, dst, :] = v_ref[src, :]

    def branch1(j, carry):
        regroup(j)
        t = jnp.minimum(j, 1)
        qs = pl.ds(pl.multiple_of(j * BLOCK, BLOCK), BLOCK)
        ks = pl.ds(pl.multiple_of(jnp.maximum(j - 1, 0) * BLOCK, BLOCK), 2 * BLOCK)
        store(acc1, qs, _pair_attention(q_ref[qs, :], k_ref[ks, :], v_ref[ks, :], bias_ref[0, t], low))
        return carry

    assert slen // BLOCK == 4 * nb4
    lax.fori_loop(0, slen // BLOCK, branch1, 0, unroll=BLOCKS_IN_FLIGHT)

    def branch4(i, carry):
        c, j = i // nb4, i % nb4
        t = jnp.minimum(j, 1)
        qs = pl.ds(pl.multiple_of(c * n4 + j * BLOCK, BLOCK), BLOCK)
        ks = pl.ds(pl.multiple_of(c * n4 + jnp.maximum(j - 1, 0) * BLOCK, BLOCK), 2 * BLOCK)
        store(acc4, qs, _pair_attention(r4[0, qs, :], r4[1, ks, :], r4[2, ks, :], bias_ref[1, t], low))
        return carry

    lax.fori_loop(0, 4 * nb4, branch4, 0, unroll=BLOCKS_IN_FLIGHT)

    def branch16(i, carry):
        res, j = i // nb16, i % nb16
        base = (res % 4) * n4 + res // 4
        t = jnp.minimum(j, 1)
        qs = pl.ds(base + 4 * BLOCK * j, BLOCK, stride=4)
        ks = pl.ds(base + 4 * BLOCK * jnp.maximum(j - 1, 0), 2 * BLOCK, stride=4)
        part = _pair_attention(r4[0, qs, :], r4[1, ks, :], r4[2, ks, :], bias_ref[2, t], low)
        store(acc4, qs, _merge(load(acc4, qs), part))
        return carry

    lax.fori_loop(0, 16 * nb16, branch16, 0, unroll=BLOCKS_IN_FLIGHT)

    def mix(i, carry):
        c, jj = i // nb4, i % nb4
        rows4 = pl.ds(pl.multiple_of(c * n4 + jj * BLOCK, BLOCK), BLOCK)
        rows1 = pl.ds(c + 4 * BLOCK * jj, BLOCK, stride=4)
        num, _, den = _merge(load(acc4, rows4), load(acc1, rows1))
        o_ref[rows1, :] = num / den
        return carry

    lax.fori_loop(0, 4 * nb4, mix, 0, unroll=4)


def _attn_call(qkv, bias):
    bsz, slen, _ = qkv.shape
    assert slen % (16 * 2 * BLOCK) == 0
    n_pairs = ATTN_W // LANES
    blk = pl.BlockSpec((None, slen, LANES), lambda b, p: (b, 0, p))
    part = lambda a: pl.BlockSpec((None, slen, LANES), lambda b, p: (b, 0, a * n_pairs + p))
    return pl.pallas_call(
        _attn_kernel,
        grid=(bsz, n_pairs),
        in_specs=[part(0), part(1), part(2),
                  pl.BlockSpec((len(DILATIONS), 2, 2 * BLOCK, 2 * BLOCK), lambda b, p: (0, 0, p, 0))],
        out_specs=blk,
        out_shape=jax.ShapeDtypeStruct((bsz, slen, ATTN_W), F32),
        scratch_shapes=[pltpu.VMEM((3, slen, LANES), F32), pltpu.VMEM((3, slen, LANES), F32),
                        pltpu.VMEM((3, slen, LANES), F32)],
        compiler_params=_params("arbitrary", "arbitrary"),
        name="prompt_attn",
    )(qkv, qkv, qkv, bias)


def _bias_lookup(rel_bias, dist):
    onehot = (_rel_bucket(dist)[..., None] == np.arange(N_BUCKETS)).astype(np.float32)
    return jnp.einsum('...i,ih->...h', onehot, rel_bias.astype(F32), precision=lax.Precision.HIGHEST)


def _prompt_bias(rel_bias):
    steps = BLOCK
    qi = np.arange(BLOCK)[:, None]
    kj = np.arange(2 * BLOCK)[None, :]
    step1 = qi + BLOCK - kj
    ok1 = (step1 >= 0) & (step1 <= steps)
    step0 = qi - kj
    ok0 = (step0 >= 0) & (kj < BLOCK)

    def table(step, ok, dil):
        b = _bias_lookup(rel_bias, np.clip(step, 0, steps) * dil)
        b = jnp.transpose(jnp.where(ok[:, :, None], b * LOG2E, NEG), (2, 0, 1))
        return b.reshape(N_HEADS * BLOCK, 2 * BLOCK)

    return jnp.stack([jnp.stack([table(step0, ok0, d), table(step1, ok1, d)]) for _, d in DILATIONS])


def _out_kernel(att_ref, gc_ref, x_ref, w_ref, lg_ref, lb_ref, *rest, alpha, rider):
    ride_refs, (y_ref, at_ref) = rest[:7], rest[7:]

    @pl.when((pl.program_id(0) == 0) & (pl.program_id(1) == 0))
    def _():
        at_ref[...] = jnp.zeros(at_ref.shape, F32)

    _ride(ride_refs, at_ref, rider)
    gated = (att_ref[...] * gc_ref[:, 0:ATTN_W].astype(F32)).astype(BF16)
    out = (jnp.dot(gc_ref[:, ATTN_W:ATTN_W + CONV_W], w_ref[0:CONV_W, :], preferred_element_type=F32)
           + jnp.dot(gated, w_ref[CONV_W:D_MODEL, :], preferred_element_type=F32))
    y_ref[...] = _layer_norm(alpha * x_ref[...] + out, lg_ref[...], lb_ref[...])


def _out_call(att, gc, x, w_out_b, ln_g2, ln_b2, layer, alpha, ride_ops, rider):
    bsz, slen, _ = x.shape
    ts = SEQ_TILE
    ride_in, ride_out, ride_shape = _ride_specs(ride_ops, rider, slen // ts)
    tile = lambda width: pl.BlockSpec((None, ts, width), lambda b, s: (b, s, 0))
    per_layer = lambda *shape: pl.BlockSpec((None,) + shape, lambda b, s: (layer,) + (0,) * len(shape))
    return pl.pallas_call(
        functools.partial(_out_kernel, alpha=alpha, rider=rider),
        grid=(bsz, slen // ts),
        in_specs=[tile(ATTN_W), tile(ATTN_W + CONV_W), tile(D_MODEL),
                  per_layer(D_MODEL, D_MODEL), per_layer(1, D_MODEL), per_layer(1, D_MODEL)] + ride_in,
        out_specs=[tile(D_MODEL), ride_out],
        out_shape=[jax.ShapeDtypeStruct((bsz, slen, D_MODEL), F32), ride_shape],
        compiler_params=_params("arbitrary", "arbitrary"),
        name="prompt_out",
    )(att, gc, x, w_out_b, ln_g2, ln_b2, *ride_ops)


def _sample_proj_kernel(x_ref, w_ref, cw_ref, s0_ref, s1_ref, qt_ref, kt_ref, vt_ref, co_ref, g_ref, cs_ref):
    xb = x_ref[...].astype(BF16)

    def col(c):
        return jnp.dot(xb, w_ref[:, c * 512:(c + 1) * 512], preferred_element_type=F32)

    u = col(2) * col(0)
    cw = cw_ref[...]
    cy = cw[0:1, :] * s0_ref[...] + cw[1:2, :] * s1_ref[...] + cw[2:3, :] * u
    co_ref[...] = col(1) * cy * _silu(col(3))
    cs_ref[...] = u
    qt_ref[...] = (col(4) * SCALE).T
    kt_ref[...] = col(5).T
    vt_ref[...] = col(6).T
    g_ref[...] = _silu(col(7))


def _sample_proj_call(xs, w_in_b, conv_w, layer, s0, s1):
    db = xs.shape[0]
    tr = jax.ShapeDtypeStruct((ATTN_W, db), F32)
    row = jax.ShapeDtypeStruct((db, ATTN_W), F32)
    full = lambda shape: pl.BlockSpec(shape, lambda i: (0,) * len(shape))
    per_layer = lambda *shape: pl.BlockSpec((None,) + shape, lambda i: (layer,) + (0,) * len(shape))
    return pl.pallas_call(
        _sample_proj_kernel,
        grid=(1,),
        in_specs=[full(xs.shape), per_layer(*w_in_b.shape[1:]), per_layer(*conv_w.shape[1:]),
                  full(s0.shape), full(s1.shape)],
        out_specs=[full(tr.shape)] * 3 + [full(row.shape)] * 3,
        out_shape=[tr, tr, tr, row, row, row],
        compiler_params=_params("arbitrary"),
        name="sample_proj",
    )(xs, w_in_b, conv_w, s0, s1)


def _sample_attend(elem, qt_ref, kt_ref, vt_ref, sb_ref, sb0_ref, kc_ref, vc_ref, at_ref):
    db = qt_ref.shape[1]
    sel = lax.broadcasted_iota(jnp.int32, (HEAD_DIM, db), 1) == elem
    for h in range(N_HEADS):
        rows = slice(h * HEAD_DIM, (h + 1) * HEAD_DIM)
        pick = lambda ref: jnp.sum(jnp.where(sel, ref[rows, :], 0.0), axis=1, keepdims=True)
        qc, kn, vn = pick(qt_ref), pick(kt_ref), pick(vt_ref)
        sc = jnp.sum(kc_ref[rows, :] * qc, axis=0, keepdims=True) + sb_ref[h:h + 1, :]
        s_self = jnp.sum(qc * kn, axis=0, keepdims=True) + sb0_ref[h:h + 1, 0:1]
        m = jnp.maximum(jnp.max(sc, axis=1, keepdims=True), s_self)
        pe = jnp.exp(sc - m)
        p_self = jnp.exp(s_self - m)
        den = jnp.sum(pe, axis=1, keepdims=True) + p_self
        num = jnp.sum(vc_ref[rows, :] * pe, axis=1, keepdims=True) + p_self * vn
        at_ref[rows, :] = jnp.where(sel, num / den, at_ref[rows, :])


class _Rider(NamedTuple):
    layer: int
    first: int
    per_step: int


def _ride(ride_refs, at_ref, rider):
    qt_ref, kt_ref, vt_ref, sb_ref, sb0_ref, kc_ref, vc_ref = ride_refs
    step = pl.program_id(0) * pl.num_programs(1) + pl.program_id(1)
    for bi in range(rider.per_step):
        _sample_attend(rider.first + step * rider.per_step + bi, qt_ref, kt_ref, vt_ref, sb_ref, sb0_ref,
                       kc_ref.at[bi], vc_ref.at[bi], at_ref)


def _ride_specs(ride_ops, rider, ns):
    ck = ride_ops[5]
    db, feat, npos = ck.shape[1], ck.shape[2], ck.shape[3]
    assert rider.first % rider.per_step == 0
    first_block = rider.first // rider.per_step
    full = lambda shape: pl.BlockSpec(shape, lambda b, s: (0,) * len(shape))
    cache = pl.BlockSpec((None, rider.per_step, feat, npos), lambda b, s: (rider.layer, first_block + b * ns + s, 0, 0))
    return ([full(op.shape) for op in ride_ops[:5]] + [cache, cache], full((feat, db)),
            jax.ShapeDtypeStruct((feat, db), F32))


def _sample_out_kernel(at_a_ref, at_b_ref, co_ref, g_ref, x_ref, w_ref, lg_ref, lb_ref, y_ref, *, alpha):
    gated = ((at_a_ref[...] + at_b_ref[...]).T * g_ref[...]).astype(BF16)
    out = (jnp.dot(co_ref[...].astype(BF16), w_ref[0:CONV_W, :], preferred_element_type=F32)
           + jnp.dot(gated, w_ref[CONV_W:D_MODEL, :], preferred_element_type=F32))
    y_ref[...] = _layer_norm(alpha * x_ref[...] + out, lg_ref[...], lb_ref[...])


def _sample_out_call(at_a, at_b, co, gate, xs, w_out_b, ln_g2, ln_b2, layer, alpha):
    full = lambda shape: pl.BlockSpec(shape, lambda i: (0,) * len(shape))
    per_layer = lambda *shape: pl.BlockSpec((None,) + shape, lambda i: (layer,) + (0,) * len(shape))
    return pl.pallas_call(
        functools.partial(_sample_out_kernel, alpha=alpha),
        grid=(1,),
        in_specs=[full(at_a.shape), full(at_b.shape), full(co.shape), full(gate.shape), full(xs.shape),
                  per_layer(D_MODEL, D_MODEL), per_layer(1, D_MODEL), per_layer(1, D_MODEL)],
        out_specs=full(xs.shape),
        out_shape=jax.ShapeDtypeStruct(xs.shape, F32),
        compiler_params=_params("arbitrary"),
        name="sample_out",
    )(at_a, at_b, co, gate, xs, w_out_b, ln_g2, ln_b2)


def _sample_bias(rel_bias, n_past):
    def table(dist):
        mult = sum(((dist % d == 0) & (dist <= w)).astype(np.float32) for w, d in DILATIONS)
        b = _bias_lookup(rel_bias, dist) + np.log(np.maximum(mult, 1.0))[:, None]
        return jnp.where((mult > 0)[:, None], b, NEG).T
    cache = table(n_past - np.arange(n_past))
    own = jnp.broadcast_to(table(np.zeros((1,), np.int32)), (N_HEADS, LANES))
    return cache, own


def kernel(x_prompt, x_sample, cache_k, cache_v, state_conv, w_in, conv_w, w_out, ln_g, ln_b, rel_bias):
    depth = w_in.shape[0]
    bsz, slen, _ = x_prompt.shape
    db, n_past = cache_k.shape[1], cache_k.shape[2]
    assert x_sample.shape[1] == 1
    alpha = (2 * depth) ** 0.25
    keep = min(MAX_WINDOW, slen)
    w_in_b = w_in.astype(BF16)
    w_out_b = w_out.astype(BF16)
    ln_g2 = ln_g.reshape(depth, 1, D_MODEL)
    ln_b2 = ln_b.reshape(depth, 1, D_MODEL)
    bias = _prompt_bias(rel_bias)

    from_fp = lambda t, lead: jnp.transpose(t.reshape(lead + (N_HEADS, HEAD_DIM, t.shape[-1])),
                                            tuple(range(len(lead))) + (len(lead) + 2, len(lead), len(lead) + 1))

    ck = jnp.transpose(cache_k, (0, 1, 3, 4, 2)).reshape(depth, db, ATTN_W, n_past)
    cv = jnp.transpose(cache_v, (0, 1, 3, 4, 2)).reshape(depth, db, ATTN_W, n_past)
    sbias, sbias0 = _sample_bias(rel_bias, n_past)
    steps = bsz * (slen // SEQ_TILE)
    per_step = db // (2 * steps)
    assert 2 * steps * per_step == db

    hp = x_prompt
    hs = x_sample.reshape(db, D_MODEL)
    kv_all = None
    cp_l, ks_l, vs_l, cs_l = [], [], [], []
    for l in range(depth):
        s0, s1 = state_conv[l, :, 0, :], state_conv[l, :, 1, :]
        qt, kts, vts, cos, gates, u = _sample_proj_call(hs, w_in_b, conv_w, l, s0, s1)
        ride_ops = (qt, kts, vts, sbias, sbias0, ck, cv)
        qkv, gc, kt_all, vt_all, cs, at_a = _proj_call(hp, w_in_b, conv_w, l, keep, ride_ops,
                                                       _Rider(l, 0, per_step), kv_all)
        kv_all = (kt_all, vt_all)
        att = _attn_call(qkv, bias)
        hp, at_b = _out_call(att, gc, hp, w_out_b, ln_g2, ln_b2, l, alpha, ride_ops,
                             _Rider(l, steps * per_step, per_step))
        hs = _sample_out_call(at_a, at_b, cos, gates, hs, w_out_b, ln_g2, ln_b2, l, alpha)
        cp_l.append(cs)
        ks_l.append(from_fp(kts, ()).reshape(db, 1, N_HEADS, HEAD_DIM))
        vs_l.append(from_fp(vts, ()).reshape(db, 1, N_HEADS, HEAD_DIM))
        cs_l.append(jnp.stack([s1, u], axis=1))

    return (hp, hs.reshape(db, 1, D_MODEL), from_fp(kv_all[0], (depth, bsz)), from_fp(kv_all[1], (depth, bsz)),
            jnp.stack(cp_l),
            jnp.stack(ks_l), jnp.stack(vs_l), jnp.stack(cs_l))
```
